```python
import jax, jax.numpy as jnp
from jax import lax
import numpy as np

D_MODEL = 2048
BATCH = 2
SEQ = 4096
DEPTH = 1

MEM_LEN = 256
EPS = 1e-6

HG_HEADS = 6
HG_DK = 128
HG_DV = 128
HG_WIDTH = HG_HEADS * HG_DK
HG_CHUNK = 64

DIL_GROUPS = ((128, 1), (512, 4), (2048, 16))
DIL_HEADS_PER_GROUP = 4
DIL_HEADS = DIL_HEADS_PER_GROUP * len(DIL_GROUPS)
DIL_HEAD_DIM = 64
DIL_WIDTH = DIL_HEADS * DIL_HEAD_DIM
DIL_QBLOCK = 128
ALIBI_MAX = 8.0

MEM_HEADS = 4
MEM_HEAD_DIM = 128
MEM_WIDTH = MEM_HEADS * MEM_HEAD_DIM

N_BRANCH = 3
MIX_WIDTH = HG_WIDTH + DIL_WIDTH + MEM_WIDTH
N_IN = 5 * HG_WIDTH + 3 * DIL_WIDTH + MEM_WIDTH + N_BRANCH * D_MODEL

PEER_HEADS = 8
PEER_NKEYS = 128
PEER_N_EXPERTS = PEER_NKEYS * PEER_NKEYS
PEER_HALF = 128
PEER_QDIM = 2 * PEER_HALF
PEER_TOPK = 16
PEER_TBLOCK = 64

kernel_name = "hybrid_hgrn2_dilated_mem_peer_encoder"


def rmsnorm(x, g):
    xf = x.astype(jnp.float32)
    y = xf * lax.rsqrt(jnp.mean(xf * xf, axis=-1, keepdims=True) + EPS)
    return (y * g.astype(jnp.float32)).astype(x.dtype)


def hgrn2_scan(q, k, v, logf):
    B, H, S, dk = q.shape
    dv = v.shape[-1]
    n = S // HG_CHUNK

    def chunks(t):
        return jnp.moveaxis(t.reshape(B, H, n, HG_CHUNK, t.shape[-1]), 2, 0)

    mask = jnp.tril(jnp.ones((HG_CHUNK, HG_CHUNK), dtype=bool))[:, :, None]

    def step(state, inp):
        qc, kc, vc, lf = inp
        b = jnp.cumsum(lf, axis=2)
        o_inter = jnp.einsum('bhtc,bhcv->bhtv', qc * jnp.exp(b), state)
        diff = b[:, :, :, None, :] - b[:, :, None, :, :]
        decay = jnp.exp(jnp.where(mask, diff, -jnp.inf))
        a = jnp.einsum('bhtsc,bhsc->bhts', qc[:, :, :, None, :] * decay, kc)
        o_intra = jnp.einsum('bhts,bhsv->bhtv', a, vc)
        b_last = b[:, :, -1:, :]
        new_state = jnp.exp(b_last[:, :, 0, :])[..., None] * state + jnp.einsum(
            'bhsc,bhsv->bhcv', kc * jnp.exp(b_last - b), vc)
        return new_state, o_inter + o_intra

    init = jnp.zeros((B, H, dk, dv), jnp.float32)
    _, ys = lax.scan(step, init, (chunks(q), chunks(k), chunks(v), chunks(logf)))
    return jnp.moveaxis(ys, 0, 2).reshape(B, H, S, dv)


def dilated_group_attn(q, k, v, slopes, dilation, n_side):
    B, Hg, S, hd = q.shape
    offs = jnp.arange(-n_side, n_side + 1) * dilation
    bias = -slopes[:, None] * jnp.abs(offs).astype(jnp.float32)[None, :]
    scale = hd ** -0.5

    def block(i):
        start = i * DIL_QBLOCK
        qb = lax.dynamic_slice_in_dim(q, start, DIL_QBLOCK, axis=2)
        pos = start + jnp.arange(DIL_QBLOCK)
        idx = pos[:, None] + offs[None, :]
        valid = (idx >= 0) & (idx < S)
        idxc = jnp.clip(idx, 0, S - 1)
        kb = k[:, :, idxc]
        vb = v[:, :, idxc]
        s = jnp.einsum('bhqd,bhqkd->bhqk', qb, kb).astype(jnp.float32) * scale + bias[None, :, None, :]
        s = jnp.where(valid[None, None], s, -1e30)
        lse = jax.nn.logsumexp(s, axis=-1)
        p = jnp.exp(s - lse[..., None]).astype(v.dtype)
        o = jnp.einsum('bhqk,bhqkd->bhqd', p, vb)
        return o, lse

    o, lse = lax.map(block, jnp.arange(S // DIL_QBLOCK))
    o = jnp.moveaxis(o, 0, 2).reshape(B, Hg, S, hd)
    lse = jnp.moveaxis(lse, 0, 2).reshape(B, Hg, S)
    return o, lse


def token_mixer(h, mem_n, w_in, b_gate, lb_f, lb_b, g_hg, w_mem_kv, w_up, w_out):
    B, S, D = h.shape
    f32 = jnp.float32
    sizes = [HG_WIDTH] * 5 + [DIL_WIDTH] * 3 + [MEM_WIDTH] + [D_MODEL] * N_BRANCH
    cuts = np.cumsum(sizes)[:-1].tolist()
    proj = h @ w_in
    (hq, hi, hff, hfb, hgate, dq, dk, dv, mq, gate_cols) = (
        *jnp.split(proj, cuts[:9], axis=-1)[:9], proj[..., cuts[8]:])

    def heads(t, nh, d):
        return t.reshape(B, S, nh, d).transpose(0, 2, 1, 3)

    def forget(z, lb):
        f = lb + (1.0 - lb) * jax.nn.sigmoid(z.astype(f32))
        return heads(1.0 - f, HG_HEADS, HG_DK), heads(jnp.log(f), HG_HEADS, HG_DK)

    q_hg = heads(hq.astype(f32), HG_HEADS, HG_DK)
    v_hg = heads(hi.astype(f32), HG_HEADS, HG_DV)
    k_f, logf_f = forget(hff, lb_f)
    k_b, logf_b = forget(hfb, lb_b)
    o_fwd = hgrn2_scan(q_hg, k_f, v_hg, logf_f)
    flip = lambda t: jnp.flip(t, axis=2)
    o_bwd = flip(hgrn2_scan(flip(q_hg), flip(k_b), flip(v_hg), flip(logf_b)))
    o_hg = o_fwd + o_bwd
    o_hg = o_hg * lax.rsqrt(jnp.mean(o_hg * o_hg, axis=-1, keepdims=True) + EPS) * g_hg.astype(f32)
    o_hg = o_hg.transpose(0, 2, 1, 3).reshape(B, S, HG_WIDTH)
    o_hg = (o_hg * jax.nn.silu(hgate.astype(f32))).astype(h.dtype)

    dq = heads(dq, DIL_HEADS, DIL_HEAD_DIM)
    dk = heads(dk, DIL_HEADS, DIL_HEAD_DIM)
    dv = heads(dv, DIL_HEADS, DIL_HEAD_DIM)
    slopes = jnp.exp2(-ALIBI_MAX * jnp.arange(1, DIL_HEADS + 1, dtype=f32) / DIL_HEADS)
    outs, lses = [], []
    for gi, (window, dil) in enumerate(DIL_GROUPS):
        sl = slice(gi * DIL_HEADS_PER_GROUP, (gi + 1) * DIL_HEADS_PER_GROUP)
        n_side = (window // 2) // dil
        o_g, lse_g = dilated_group_attn(dq[:, sl], dk[:, sl], dv[:, sl], slopes[sl], dil, n_side)
        outs.append(o_g)
        lses.append(lse_g)
    alpha = jax.nn.softmax(jnp.stack(lses, axis=0), axis=0)
    o_dil = jnp.concatenate(
        [alpha[gi][..., None].astype(h.dtype) * outs[gi] for gi in range(len(DIL_GROUPS))], axis=1)
    o_dil = o_dil.transpose(0, 2, 1, 3).reshape(B, S, DIL_WIDTH)

    q_m = heads(mq, MEM_HEADS, MEM_HEAD_DIM)
    kv = mem_n @ w_mem_kv
    M = kv.shape[1]
    k_m = kv[..., :MEM_WIDTH].reshape(B, M, MEM_HEADS, MEM_HEAD_DIM).transpose(0, 2, 1, 3)
    v_m = kv[..., MEM_WIDTH:].reshape(B, M, MEM_HEADS, MEM_HEAD_DIM).transpose(0, 2, 1, 3)
    s_m = jnp.einsum('bhqd,bhkd->bhqk', q_m, k_m).astype(f32) * (MEM_HEAD_DIM ** -0.5)
    p_m = jax.nn.softmax(s_m, axis=-1).astype(v_m.dtype)
    o_mem = jnp.einsum('bhqk,bhkd->bhqd', p_m, v_m).transpose(0, 2, 1, 3).reshape(B, S, MEM_WIDTH)

    gates = jax.nn.sigmoid((gate_cols + b_gate).astype(f32)).reshape(B, S, N_BRANCH, D).astype(h.dtype)
    w_up_hg, w_up_dil, w_up_mem = jnp.split(w_up, [HG_WIDTH, HG_WIDTH + DIL_WIDTH], axis=0)
    merged = (gates[:, :, 0] * (o_hg @ w_up_hg)
              + gates[:, :, 1] * (o_dil @ w_up_dil)
              + gates[:, :, 2] * (o_mem @ w_up_mem))
    return merged @ w_out


def peer_ffn(h, w_q, sub_keys, expert_u, expert_v):
    B, S, D = h.shape
    T = B * S
    xt = h.reshape(T, D)
    q = (xt @ w_q).reshape(T, PEER_HEADS, 2, PEER_HALF)
    s = jnp.einsum('thpc,pnc->thpn', q, sub_keys).astype(jnp.float32)
    top_s, top_i = lax.top_k(s, PEER_TOPK)
    cand_s = top_s[:, :, 0, :, None] + top_s[:, :, 1, None, :]
    cand_i = top_i[:, :, 0, :, None] * PEER_NKEYS + top_i[:, :, 1, None, :]
    kk = PEER_TOPK * PEER_TOPK
    best_s, best_pos = lax.top_k(cand_s.reshape(T, PEER_HEADS, kk), PEER_TOPK)
    expert_idx = jnp.take_along_axis(cand_i.reshape(T, PEER_HEADS, kk), best_pos, axis=-1)
    gate = jax.nn.softmax(best_s, axis=-1).astype(h.dtype)
    nb = T // PEER_TBLOCK
    n_sel = PEER_HEADS * PEER_TOPK
    xb = xt.reshape(nb, PEER_TBLOCK, D)
    ib = expert_idx.reshape(nb, PEER_TBLOCK, n_sel)
    gb = gate.reshape(nb, PEER_TBLOCK, n_sel)

    def block(args):
        xs, idx, g = args
        u_sel = expert_u[idx]
        a = jnp.einsum('tkd,td->tk', u_sel, xs)
        hid = jax.nn.gelu(a, approximate=False) * g
        return jnp.einsum('tk,tkd->td', hid, expert_v[idx])

    out = lax.map(block, (xb, ib, gb))
    return out.reshape(B, S, D)


def setup_inputs(seed: int = 0) -> dict:
    key = jax.random.key(seed)
    ks = jax.random.split(key, 20)
    f32 = jnp.float32

    def nrm(k, shape, scale):
        return jax.random.normal(k, shape, f32) * scale

    D = D_MODEL
    return {
        "x": nrm(ks[0], (BATCH, SEQ, D), 1.0),
        "mem": nrm(ks[1], (BATCH, MEM_LEN, D), 1.0),
        "g_mix": 1.0 + nrm(ks[2], (DEPTH, D), 0.02),
        "w_in": nrm(ks[3], (DEPTH, D, N_IN), D ** -0.5),
        "b_gate": nrm(ks[4], (DEPTH, N_BRANCH * D), 0.02),
        "lb_fwd_logits": nrm(ks[5], (DEPTH + 1, HG_WIDTH), 0.1),
        "lb_bwd_logits": nrm(ks[6], (DEPTH + 1, HG_WIDTH), 0.1),
        "g_hgrn": 1.0 + nrm(ks[7], (DEPTH, HG_DV), 0.02),
        "w_mem_kv": nrm(ks[8], (DEPTH, D, 2 * MEM_WIDTH), D ** -0.5),
        "g_mem": 1.0 + nrm(ks[9], (DEPTH, D), 0.02),
        "w_up": nrm(ks[10], (DEPTH, MIX_WIDTH, D), MIX_WIDTH ** -0.5),
        "w_out": nrm(ks[11], (DEPTH, D, D), D ** -0.5),
        "g_ffn": 1.0 + nrm(ks[12], (DEPTH, D), 0.02),
        "w_peer_q": nrm(ks[13], (DEPTH, D, PEER_HEADS * PEER_QDIM), D ** -0.5),
        "peer_sub_keys": nrm(ks[14], (DEPTH, 2, PEER_NKEYS, PEER_HALF), PEER_HALF ** -0.5),
        "peer_u": nrm(ks[15], (DEPTH, PEER_N_EXPERTS, D), D ** -0.5),
        "peer_v": nrm(ks[16], (DEPTH, PEER_N_EXPERTS, D), 0.5),
        "g_final": 1.0 + nrm(ks[17], (D,), 0.02),
    }


def reference(x, mem, g_mix, w_in, b_gate, lb_fwd_logits, lb_bwd_logits, g_hgrn, w_mem_kv, g_mem,
              w_up, w_out, g_ffn, w_peer_q, peer_sub_keys, peer_u, peer_v, g_final):
    lb_f_all = jnp.cumsum(jax.nn.softmax(lb_fwd_logits.astype(jnp.float32), axis=0), axis=0)
    lb_b_all = jnp.cumsum(jax.nn.softmax(lb_bwd_logits.astype(jnp.float32), axis=0), axis=0)
    for l in range(DEPTH):
        h = rmsnorm(x, g_mix[l])
        mem_n = rmsnorm(mem, g_mem[l])
        x = x + token_mixer(h, mem_n, w_in[l], b_gate[l], lb_f_all[l], lb_b_all[l], g_hgrn[l],
                            w_mem_kv[l], w_up[l], w_out[l])
        x = x + peer_ffn(rmsnorm(x, g_ffn[l]), w_peer_q[l], peer_sub_keys[l], peer_u[l], peer_v[l])
    return rmsnorm(x, g_final)
```

```python
import functools

import numpy as np
import jax
import jax.numpy as jnp
from jax import lax
from jax.experimental import pallas as pl
from jax.experimental.pallas import tpu as pltpu

F32 = jnp.float32
BF16 = jnp.bfloat16

EPS = 1e-6
LANE = 128
VMEM_LIMIT = 56 * 1024 * 1024

D_MODEL = 2048
HG_HEADS, HG_DK = 6, 128
HG_WIDTH = HG_HEADS * HG_DK
HG_CHUNK = 64
DIL_GROUPS = ((128, 1), (512, 4), (2048, 16))
DIL_HPG, DIL_HD = 4, 64
DIL_GW = DIL_HPG * DIL_HD
DIL_WIDTH = DIL_GW * len(DIL_GROUPS)
DIL_HEADS = DIL_HPG * len(DIL_GROUPS)
ALIBI_MAX = 8.0
MEM_HEADS, MEM_HD = 4, 128
MEM_WIDTH = MEM_HEADS * MEM_HD
N_BRANCH = 3
HG_COLS = 5 * HG_WIDTH
AT_COLS = 3 * DIL_WIDTH + MEM_WIDTH
GATE_COLS = N_BRANCH * D_MODEL
PEER_HEADS, PEER_NKEYS, PEER_HALF, PEER_TOPK = 8, 128, 128, 16

NT_DIMS = (((1,), (1,)), ((), ()))
TN_DIMS = (((0,), (0,)), ((), ()))


def _params(*sem):
    return pltpu.CompilerParams(dimension_semantics=sem, vmem_limit_bytes=VMEM_LIMIT)


def _rms(x, g):
    return x * lax.rsqrt(jnp.mean(x * x, axis=-1, keepdims=True) + EPS) * g


def _norm_matmul_kernel(x_ref, g_ref, w_ref, *rest, sigmoid_bias):
    if sigmoid_bias:
        b_ref, o_ref, h_ref = rest
    else:
        o_ref, h_ref = rest

    @pl.when(pl.program_id(1) == 0)
    def _():
        h_ref[...] = _rms(x_ref[...], g_ref[...]).astype(h_ref.dtype)

    acc = jnp.dot(h_ref[...], w_ref[...], preferred_element_type=F32)
    if sigmoid_bias:
        acc = jax.nn.sigmoid(acc + b_ref[...])
    o_ref[...] = acc.astype(o_ref.dtype)


def _norm_matmul(x, g, w, tn, out_dtype, bias=None, tm=1024):
    t, d = x.shape
    ncols = w.shape[1]
    tm = min(tm, t)
    assert t % tm == 0 and ncols % tn == 0
    in_specs = [
        pl.BlockSpec((tm, d), lambda i, j: (i, 0)),
        pl.BlockSpec((1, d), lambda i, j: (0, 0)),
        pl.BlockSpec((d, tn), lambda i, j: (0, j)),
    ]
    args = [x, g.reshape(1, d), w]
    if bias is not None:
        in_specs.append(pl.BlockSpec((1, tn), lambda i, j: (0, j)))
        args.append(bias.reshape(1, ncols))
    return pl.pallas_call(
        functools.partial(_norm_matmul_kernel, sigmoid_bias=bias is not None),
        grid=(t // tm, ncols // tn),
        in_specs=in_specs,
        out_specs=pl.BlockSpec((tm, tn), lambda i, j: (i, j)),
        out_shape=jax.ShapeDtypeStruct((t, ncols), out_dtype),
        scratch_shapes=[pltpu.VMEM((tm, d), BF16)],
        compiler_params=_params("arbitrary", "arbitrary"),
        name="norm_matmul",
    )(*args)


def _hgrn_levels(c):
    ms = []
    m = 1
    while m <= c:
        ms.append(m)
        m *= 2
    return ms


def _hgrn_constants(c):
    t = np.arange(c)[:, None]
    r = np.arange(c)[None, :]
    sums = {"f": [], "b": []}
    masks = {"f": [np.eye(c)], "b": [np.eye(c)]}
    for m in _hgrn_levels(c):
        same = (t // m) == (r // m)
        sums["f"] += [same & (r <= t), same & (r > t)]
        sums["b"] += [same & (r >= t), same & (r < t)]
        if m < c:
            blk = (t // (2 * m)) == (r // (2 * m))
            t_hi, r_hi = (t // m) % 2 == 1, (r // m) % 2 == 1
            masks["f"].append(blk & t_hi & ~r_hi)
            masks["b"].append(blk & ~t_hi & r_hi)
    out = []
    for d in ("f", "b"):
        out.append(jnp.asarray(np.concatenate(sums[d], axis=0).astype(np.float32), BF16))
        out.append(jnp.asarray(np.stack(masks[d]).astype(np.float32), F32))
    return out


def _split3(x):
    hi = x.astype(BF16)
    r1 = x - hi.astype(F32)
    mid = r1.astype(BF16)
    lo = (r1 - mid.astype(F32)).astype(BF16)
    return hi, mid, lo


def _hgrn_chunk(q, v, z, lb, st, sums_ref, masks_ref, *, c, total_row):
    f = lb + (1.0 - lb) * jax.nn.sigmoid(z)
    lf = jnp.log(f)
    k = 1.0 - f
    cs3 = jnp.dot(sums_ref[...], jnp.concatenate(_split3(lf), axis=1), preferred_element_type=F32)
    e = jnp.exp(cs3[:, :LANE] + cs3[:, LANE:2 * LANE] + cs3[:, 2 * LANE:])
    vb = v.astype(BF16)
    a = masks_ref[0] * lax.dot_general(q.astype(BF16), k.astype(BF16), NT_DIMS, preferred_element_type=F32)
    nlev = len(_hgrn_levels(c))
    for li in range(nlev):
        eq = e[2 * li * c:(2 * li + 1) * c]
        ek = e[(2 * li + 1) * c:(2 * li + 2) * c]
        qd = (q * eq).astype(BF16)
        kd = (k * ek).astype(BF16)
        if li < nlev - 1:
            a = a + masks_ref[li + 1] * lax.dot_general(qd, kd, NT_DIMS, preferred_element_type=F32)
        else:
            o = lax.dot_general(qd, st.astype(BF16), NT_DIMS, preferred_element_type=F32)
            o = o + jnp.dot(a.astype(BF16), vb, preferred_element_type=F32)
            decay = eq[total_row:total_row + 1, :]
            st = st * decay + lax.dot_general(vb, kd, TN_DIMS, preferred_element_type=F32)
    return o, st


def _hgrn_kernel(qf_ref, vf_ref, zf_ref, qb_ref, vb_ref, zb_ref, lbf_ref, lbb_ref,
                 sf_ref, mf_ref, sb_ref, mb_ref, of_ref, ob_ref, stf_ref, stb_ref, *, c, tb):
    @pl.when(pl.program_id(2) == 0)
    def _():
        stf_ref[...] = jnp.zeros_like(stf_ref)
        stb_ref[...] = jnp.zeros_like(stb_ref)

    def lower_bound(ref):
        lg = ref[...]
        ex = jnp.exp(lg - jnp.max(lg, axis=0, keepdims=True))
        return ex[0:1, :] / jnp.sum(ex, axis=0, keepdims=True)

    lbf, lbb = lower_bound(lbf_ref), lower_bound(lbb_ref)
    st_f, st_b = stf_ref[...], stb_ref[...]
    n = tb // c
    for ci in range(n):
        sl = slice(ci * c, (ci + 1) * c)
        o, st_f = _hgrn_chunk(qf_ref[sl, :], vf_ref[sl, :], zf_ref[sl, :], lbf, st_f, sf_ref, mf_ref,
                              c=c, total_row=c - 1)
        of_ref[sl, :] = o
        sl = slice((n - 1 - ci) * c, (n - ci) * c)
        o, st_b = _hgrn_chunk(qb_ref[sl, :], vb_ref[sl, :], zb_ref[sl, :], lbb, st_b, sb_ref, mb_ref,
                              c=c, total_row=0)
        ob_ref[sl, :] = o
    stf_ref[...] = st_f
    stb_ref[...] = st_b


def _hgrn_scan(proj, lbf_logits, lbb_logits, batch, seq, tb=512, c=HG_CHUNK):
    t = batch * seq
    tb = min(tb, seq)
    nb = seq // tb
    assert seq % tb == 0 and tb % c == 0
    hh = HG_HEADS
    sums_f, masks_f, sums_b, masks_b = _hgrn_constants(c)
    nrow = lbf_logits.shape[0]

    def fwd(col):
        return pl.BlockSpec((tb, HG_DK), lambda b, h, i: (b * nb + i, col * hh + h))

    def bwd(col):
        return pl.BlockSpec((tb, HG_DK), lambda b, h, i: (b * nb + nb - 1 - i, col * hh + h))

    def const(a):
        return pl.BlockSpec(a.shape, lambda b, h, i: (0,) * a.ndim)

    lb_spec = pl.BlockSpec((nrow, HG_DK), lambda b, h, i: (0, h))
    out_sds = jax.ShapeDtypeStruct((t, HG_WIDTH), F32)
    return pl.pallas_call(
        functools.partial(_hgrn_kernel, c=c, tb=tb),
        grid=(batch, hh, nb),
        in_specs=[fwd(0), fwd(1), fwd(2), bwd(0), bwd(1), bwd(3), lb_spec, lb_spec,
                  const(sums_f), const(masks_f), const(sums_b), const(masks_b)],
        out_specs=[pl.BlockSpec((tb, HG_DK), lambda b, h, i: (b * nb + i, h)),
                   pl.BlockSpec((tb, HG_DK), lambda b, h, i: (b * nb + nb - 1 - i, h))],
        out_shape=[out_sds, out_sds],
        scratch_shapes=[pltpu.VMEM((HG_DK, HG_DK), F32), pltpu.VMEM((HG_DK, HG_DK), F32)],
        compiler_params=_params("arbitrary", "arbitrary", "arbitrary"),
        name="hgrn_scan",
    )(proj, proj, proj, proj, proj, proj, lbf_logits, lbb_logits, sums_f, masks_f, sums_b, masks_b)


def _dil_kernel(q_ref, k_ref, v_ref, o_ref, lse_ref, *, lq, length, win, n_side, slopes):
    q0 = pl.program_id(2) * lq
    start = pl.multiple_of(jnp.clip(q0 - n_side, 0, length - win), n_side)
    q = q_ref[...]
    kw = k_ref[pl.ds(start, win), :]
    vw = v_ref[pl.ds(start, win), :]
    qpos = q0 + lax.broadcasted_iota(jnp.int32, (lq, win), 0)
    kpos = start + lax.broadcasted_iota(jnp.int32, (lq, win), 1)
    dist = jnp.abs(kpos - qpos)
    valid = dist <= n_side
    distf = dist.astype(F32)
    outs, lses = [], []
    for i, slope in enumerate(slopes):
        cs = slice(i * DIL_HD, (i + 1) * DIL_HD)
        s = lax.dot_general(q[:, cs], kw[:, cs], NT_DIMS, preferred_element_type=F32)
        s = s * (DIL_HD ** -0.5) - slope * distf
        s = jnp.where(valid, s, -1e30)
        m = jnp.max(s, axis=-1, keepdims=True)
        p = jnp.exp(s - m)
        l = jnp.sum(p, axis=-1, keepdims=True)
        o = jnp.dot(p.astype(BF16), vw[:, cs], preferred_element_type=F32) / l
        outs.append(o)
        lses.append(jnp.broadcast_to(m + jnp.log(l), (lq, DIL_HD)))
    o_ref[...] = jnp.concatenate(outs, axis=1)
    lse_ref[...] = jnp.concatenate(lses, axis=1)


def _dilated_group(proj_at, gi, batch, seq):
    window, r = DIL_GROUPS[gi]
    n_side = (window // 2) // r
    t = batch * seq
    length = seq // r
    lq = min(256, length)
    win = min(length, lq + 2 * n_side)
    nq = length // lq
    assert seq % r == 0 and length % lq == 0 and lq % n_side == 0 and (length - win) % n_side == 0
    nblk = AT_COLS // DIL_GW
    assert AT_COLS % DIL_GW == 0
    ngrp = len(DIL_GROUPS)
    heads = np.arange(gi * DIL_HPG, (gi + 1) * DIL_HPG)
    slopes = tuple(float(s) * r for s in np.exp2(-ALIBI_MAX * (heads + 1) / DIL_HEADS))
    x = proj_at.reshape(t // r, r * AT_COLS)

    def kv_spec(which):
        return pl.BlockSpec((length, DIL_GW), lambda b, rho, qi: (b, rho * nblk + which * ngrp + gi))

    out_spec = pl.BlockSpec((lq, DIL_GW), lambda b, rho, qi: (b * nq + qi, rho))
    out_sds = jax.ShapeDtypeStruct((t // r, r * DIL_GW), F32)
    o, lse = pl.pallas_call(
        functools.partial(_dil_kernel, lq=lq, length=length, win=win, n_side=n_side, slopes=slopes),
        grid=(batch, r, nq),
        in_specs=[pl.BlockSpec((lq, DIL_GW), lambda b, rho, qi: (b * nq + qi, rho * nblk + gi)),
                  kv_spec(1), kv_spec(2)],
        out_specs=[out_spec, out_spec],
        out_shape=[out_sds, out_sds],
        compiler_params=_params("arbitrary", "arbitrary", "arbitrary"),
        name=f"dilated_attn_{gi}",
    )(x, x, x)
    return o.reshape(t, DIL_GW), lse.reshape(t, DIL_GW)


def _mem_attn_kernel(q_ref, k_ref, v_ref, o_ref):
    q, k, v = q_ref[...], k_ref[...], v_ref[...]
    outs = []
    for i in range(q.shape[1] // MEM_HD):
        cs = slice(i * MEM_HD, (i + 1) * MEM_HD)
        s = lax.dot_general(q[:, cs], k[:, cs], NT_DIMS, preferred_element_type=F32) * (MEM_HD ** -0.5)
        p = jnp.exp(s - jnp.max(s, axis=-1, keepdims=True))
        l = jnp.sum(p, axis=-1, keepdims=True)
        outs.append(jnp.dot(p.astype(BF16), v[:, cs], preferred_element_type=F32) / l)
    o_ref[...] = jnp.concatenate(outs, axis=1).astype(o_ref.dtype)


def _mem_attn(proj_at, kv, batch, seq, tq=1024):
    t = batch * seq
    tq = min(tq, seq)
    nq = seq // tq
    m = kv.shape[0] // batch
    bw = 2 * MEM_HD
    npair = MEM_WIDTH // bw
    q_off = (3 * DIL_WIDTH) // bw
    assert (3 * DIL_WIDTH) % bw == 0 and seq % tq == 0
    return pl.pallas_call(
        _mem_attn_kernel,
        grid=(batch, nq, npair),
        in_specs=[pl.BlockSpec((tq, bw), lambda b, i, hp: (b * nq + i, q_off + hp)),
                  pl.BlockSpec((m, bw), lambda b, i, hp: (b, hp)),
                  pl.BlockSpec((m, bw), lambda b, i, hp: (b, npair + hp))],
        out_specs=pl.BlockSpec((tq, bw), lambda b, i, hp: (b * nq + i, hp)),
        out_shape=jax.ShapeDtypeStruct((t, MEM_WIDTH), BF16),
        compiler_params=_params("arbitrary", "arbitrary", "arbitrary"),
        name="mem_attn",
    )(proj_at, kv, kv)


def _merge_kernel(of_ref, ob_ref, hg_ref, ghg_ref, od0_ref, od1_ref, od2_ref, l0_ref, l1_ref, l2_ref,
                  om_ref, gate_ref, wup_ref, wout_ref, x_ref, gffn_ref, x1_ref, h2_ref):
    o = of_ref[...] + ob_ref[...]
    ghg = ghg_ref[...]
    parts = [_rms(o[:, h * HG_DK:(h + 1) * HG_DK], ghg) for h in range(HG_HEADS)]
    gz = hg_ref[...]
    o_hg = (jnp.concatenate(parts, axis=1) * (gz * jax.nn.sigmoid(gz))).astype(BF16)

    l0, l1, l2 = l0_ref[...], l1_ref[...], l2_ref[...]
    mx = jnp.maximum(jnp.maximum(l0, l1), l2)
    e0, e1, e2 = jnp.exp(l0 - mx), jnp.exp(l1 - mx), jnp.exp(l2 - mx)
    inv = 1.0 / (e0 + e1 + e2)
    o_dil = jnp.concatenate([od0_ref[...] * (e0 * inv), od1_ref[...] * (e1 * inv), od2_ref[...] * (e2 * inv)],
                            axis=1).astype(BF16)

    d = x_ref.shape[1]
    y_hg = jnp.dot(o_hg, wup_ref[0:HG_WIDTH, :], preferred_element_type=F32)
    merged = gate_ref[:, 0:d].astype(F32) * y_hg
    y_dil = jnp.dot(o_dil, wup_ref[HG_WIDTH:HG_WIDTH + DIL_WIDTH, :], preferred_element_type=F32)
    merged = merged + gate_ref[:, d:2 * d].astype(F32) * y_dil
    y_mem = jnp.dot(om_ref[...], wup_ref[HG_WIDTH + DIL_WIDTH:, :], preferred_element_type=F32)
    merged = merged + gate_ref[:, 2 * d:].astype(F32) * y_mem
    x1 = x_ref[...] + jnp.dot(merged.astype(BF16), wout_ref[...], preferred_element_type=F32)
    x1_ref[...] = x1
    h2_ref[...] = _rms(x1, gffn_ref[...]).astype(h2_ref.dtype)


def _merge(o_f, o_b, proj_hg, g_hg, dil, o_mem, gates, w_up, w_out, x, g_ffn, tm=256):
    t, d = x.shape
    tm = min(tm, t)
    assert t % tm == 0
    row = lambda w, col=0: pl.BlockSpec((tm, w), lambda i: (i, col))
    const = lambda a: pl.BlockSpec(a.shape, lambda i: (0, 0), pipeline_mode=pl.Buffered(1))
    (od0, l0), (od1, l1), (od2, l2) = dil
    g_hg2, g_ffn2 = g_hg.reshape(1, HG_DK), g_ffn.reshape(1, d)
    return pl.pallas_call(
        _merge_kernel,
        grid=(t // tm,),
        in_specs=[row(HG_WIDTH), row(HG_WIDTH), row(HG_WIDTH, 4), const(g_hg2),
                  row(DIL_GW), row(DIL_GW), row(DIL_GW), row(DIL_GW), row(DIL_GW), row(DIL_GW),
                  row(MEM_WIDTH), row(GATE_COLS), const(w_up), const(w_out), row(d), const(g_ffn2)],
        out_specs=[row(d), row(d)],
        out_shape=[jax.ShapeDtypeStruct((t, d), F32), jax.ShapeDtypeStruct((t, d), BF16)],
        compiler_params=_params("arbitrary"),
        name="merge",
    )(o_f, o_b, proj_hg, g_hg2, od0, od1, od2, l0, l1, l2, o_mem, gates, w_up, w_out, x, g_ffn2)


N_TOP = PEER_TOPK + 1
TOP_ROWS = 24


def _top_values(s):
    rid = lax.broadcasted_iota(jnp.int32, (TOP_ROWS, s.shape[1]), 0)
    top = jnp.full((TOP_ROWS, s.shape[1]), -jnp.inf, F32)
    rows = []
    for r in range(N_TOP):
        m = jnp.max(s, axis=0, keepdims=True)
        rows.append(m)
        top = jnp.where(rid == r, m, top)
        s = jnp.where(s == m, -jnp.inf, s)
    return top, rows


def _peer_topk_kernel(h2_ref, wq_ref, keys_ref, thr_ref, w1_ref, s2_ref, e2_ref):
    h2 = h2_ref[...]
    qd = 2 * PEER_HALF
    for h in range(PEER_HEADS):
        q_t = lax.dot_general(wq_ref[h * qd:(h + 1) * qd, :], h2, NT_DIMS, preferred_element_type=F32)
        s1 = jnp.dot(keys_ref[0], q_t[:PEER_HALF].astype(BF16), preferred_element_type=F32)
        s2 = jnp.dot(keys_ref[1], q_t[PEER_HALF:].astype(BF16), preferred_element_type=F32)
        top1, rows1 = _top_values(s1)
        top2, rows2 = _top_values(s2)
        cands = [rows1[0] + top2]
        for a in range(1, 8):
            cands.append(rows1[a] + top2[0:8])
        cands.append(top1[8:TOP_ROWS] + rows2[0])
        cand = jnp.concatenate(cands, axis=0)
        best = []
        for r in range(N_TOP):
            m = jnp.max(cand, axis=0, keepdims=True)
            best.append(m)
            cand = jnp.where(cand == m, -jnp.inf, cand)
        z = jnp.ones_like(best[0])
        for r in range(1, PEER_TOPK):
            z = z + jnp.exp(best[r] - best[0])
        tau = 0.5 * (best[PEER_TOPK - 1] + best[PEER_TOPK])
        thr_ref[h] = tau - s1
        w1_ref[h] = jnp.exp(s1 - rows1[0]) / z
        s2_ref[h] = s2
        e2_ref[h] = jnp.exp(s2 - rows2[0])


def _peer_topk(h2, wq_t, keys, tb=256):
    t, d = h2.shape
    tb = min(tb, t)
    assert t % tb == 0
    spec = pl.BlockSpec((PEER_HEADS, PEER_NKEYS, tb), lambda i: (0, 0, i))
    sds = jax.ShapeDtypeStruct((PEER_HEADS, PEER_NKEYS, t), F32)
    return pl.pallas_call(
        _peer_topk_kernel,
        grid=(t // tb,),
        in_specs=[pl.BlockSpec((tb, d), lambda i: (i, 0)),
                  pl.BlockSpec(wq_t.shape, lambda i: (0, 0)),
                  pl.BlockSpec(keys.shape, lambda i: (0, 0, 0))],
        out_specs=[spec] * 4,
        out_shape=[sds] * 4,
        compiler_params=_params("arbitrary"),
        name="peer_topk",
    )(h2, wq_t, keys)


def _peer_ffn_kernel(h2_ref, u_ref, vt_ref, thr_ref, w1_ref, s2_ref, e2_ref, o_ref, a_ref, hid_ref, *, eb):
    e_idx = pl.program_id(1)

    @pl.when(e_idx == 0)
    def _():
        o_ref[...] = jnp.zeros_like(o_ref)

    a_ref[...] = lax.dot_general(u_ref[...], h2_ref[...], NT_DIMS, preferred_element_type=F32)

    def body(ii, carry):
        r0 = pl.multiple_of(ii * PEER_NKEYS, PEER_NKEYS)
        g = None
        for h in range(PEER_HEADS):
            thr = thr_ref[h, pl.ds(ii, 1), :]
            w1 = w1_ref[h, pl.ds(ii, 1), :]
            term = jnp.where(s2_ref[h] >= thr, e2_ref[h] * w1, 0.0)
            g = term if g is None else g + term
        a = a_ref[pl.ds(r0, PEER_NKEYS), :]
        gelu = 0.5 * a * (1.0 + lax.erf(a * (0.5 ** 0.5)))
        hid_ref[pl.ds(r0, PEER_NKEYS), :] = (gelu * g).astype(hid_ref.dtype)
        return carry

    lax.fori_loop(0, eb // PEER_NKEYS, body, 0)
    o_ref[...] += jnp.dot(vt_ref[...], hid_ref[...], preferred_element_type=F32)


def _peer_ffn(h2, u, v_t, thr, w1, s2, e2, tb=512, eb=1024):
    t, d = h2.shape
    ne = u.shape[0]
    tb, eb = min(tb, t), min(eb, ne)
    assert t % tb == 0 and ne % eb == 0 and eb % PEER_NKEYS == 0
    ni = eb // PEER_NKEYS
    tok = pl.BlockSpec((PEER_HEADS, PEER_NKEYS, tb), lambda i, e: (0, 0, i))
    sel = pl.BlockSpec((PEER_HEADS, ni, tb), lambda i, e: (0, e, i))
    return pl.pallas_call(
        functools.partial(_peer_ffn_kernel, eb=eb),
        grid=(t // tb, ne // eb),
        in_specs=[pl.BlockSpec((tb, d), lambda i, e: (i, 0)),
                  pl.BlockSpec((eb, d), lambda i, e: (e, 0)),
                  pl.BlockSpec((d, eb), lambda i, e: (0, e)),
                  sel, sel, tok, tok],
        out_specs=pl.BlockSpec((d, tb), lambda i, e: (0, i)),
        out_shape=jax.ShapeDtypeStruct((d, t), F32),
        scratch_shapes=[pltpu.VMEM((eb, tb), F32), pltpu.VMEM((eb, tb), BF16)],
        compiler_params=_params("arbitrary", "arbitrary"),
        name="peer_ffn",
    )(h2, u, v_t, thr, w1, s2, e2)


def _final_kernel(x1_ref, pt_ref, g_ref, o_ref):
    o_ref[...] = _rms(x1_ref[...] + pt_ref[...].T, g_ref[...])


def _final(x1, peer_t, g, tb=512):
    t, d = x1.shape
    tb = min(tb, t)
    assert t % tb == 0
    return pl.pallas_call(
        _final_kernel,
        grid=(t // tb,),
        in_specs=[pl.BlockSpec((tb, d), lambda i: (i, 0)),
                  pl.BlockSpec((d, tb), lambda i: (0, i)),
                  pl.BlockSpec((1, d), lambda i: (0, 0))],
        out_specs=pl.BlockSpec((tb, d), lambda i: (i, 0)),
        out_shape=jax.ShapeDtypeStruct((t, d), F32),
        compiler_params=_params("arbitrary"),
        name="final_norm",
    )(x1, peer_t, g.reshape(1, d))


def kernel(x, mem, g_mix, w_in, b_gate, lb_fwd_logits, lb_bwd_logits, g_hgrn, w_mem_kv, g_mem, w_up, w_out,
           g_ffn, w_peer_q, peer_sub_keys, peer_u, peer_v, g_final):
    batch, seq, d = x.shape
    assert d == D_MODEL and w_in.shape[0] == 1, "one layer"
    t = batch * seq
    xt = x.reshape(t, d)
    w_hg = w_in[0, :, :HG_COLS].astype(BF16)
    w_at = w_in[0, :, HG_COLS:HG_COLS + AT_COLS].astype(BF16)
    w_gate = w_in[0, :, HG_COLS + AT_COLS:].astype(BF16)

    proj_hg = _norm_matmul(xt, g_mix[0], w_hg, 768, F32)
    proj_at = _norm_matmul(xt, g_mix[0], w_at, 1408, BF16)
    gates = _norm_matmul(xt, g_mix[0], w_gate, 1024, BF16, bias=b_gate[0])

    o_f, o_b = _hgrn_scan(proj_hg, lb_fwd_logits, lb_bwd_logits, batch, seq)
    dil = [_dilated_group(proj_at, gi, batch, seq) for gi in range(len(DIL_GROUPS))]
    kv = _norm_matmul(mem.reshape(-1, d), g_mem[0], w_mem_kv[0].astype(BF16), 2 * MEM_WIDTH, BF16)
    o_mem = _mem_attn(proj_at, kv, batch, seq)

    x1, h2 = _merge(o_f, o_b, proj_hg, g_hgrn[0], dil, o_mem, gates, w_up[0].astype(BF16), w_out[0].astype(BF16),
                    xt, g_ffn[0])

    thr, w1, s2, e2 = _peer_topk(h2, w_peer_q[0].T.astype(BF16), peer_sub_keys[0].astype(BF16))
    peer_t = _peer_ffn(h2, peer_u[0].astype(BF16), peer_v[0].T.astype(BF16), thr, w1, s2, e2)
    return _final(x1, peer_t, g_final).reshape(batch, seq, d)
```

```python
import functools

import numpy as np
import jax
import jax.numpy as jnp
from jax import lax
from jax.experimental import pallas as pl
from jax.experimental.pallas import tpu as pltpu

F32 = jnp.float32
BF16 = jnp.bfloat16

EPS = 1e-6
LANE = 128
VMEM_LIMIT = 56 * 1024 * 1024

D_MODEL = 2048
HG_HEADS, HG_DK = 6, 128
HG_WIDTH = HG_HEADS * HG_DK
HG_CHUNK = 64
DIL_GROUPS = ((128, 1), (512, 4), (2048, 16))
DIL_HPG, DIL_HD = 4, 64
DIL_GW = DIL_HPG * DIL_HD
DIL_WIDTH = DIL_GW * len(DIL_GROUPS)
DIL_HEADS = DIL_HPG * len(DIL_GROUPS)
ALIBI_MAX = 8.0
MEM_HEADS, MEM_HD = 4, 128
MEM_WIDTH = MEM_HEADS * MEM_HD
N_BRANCH = 3
HG_COLS = 5 * HG_WIDTH
AT_COLS = 3 * DIL_WIDTH + MEM_WIDTH
GATE_COLS = N_BRANCH * D_MODEL
PEER_HEADS, PEER_NKEYS, PEER_HALF, PEER_TOPK = 8, 128, 128, 16

NT_DIMS = (((1,), (1,)), ((), ()))
TN_DIMS = (((0,), (0,)), ((), ()))


def _params(*sem):
    return pltpu.CompilerParams(dimension_semantics=sem, vmem_limit_bytes=VMEM_LIMIT)


def _rms(x, g):
    return x * lax.rsqrt(jnp.mean(x * x, axis=-1, keepdims=True) + EPS) * g


def _norm_matmul_kernel(x_ref, g_ref, w_ref, *rest, sigmoid_bias):
    if sigmoid_bias:
        b_ref, o_ref, h_ref = rest
    else:
        o_ref, h_ref = rest

    @pl.when(pl.program_id(1) == 0)
    def _():
        h_ref[...] = _rms(x_ref[...], g_ref[...]).astype(h_ref.dtype)

    acc = jnp.dot(h_ref[...], w_ref[...], preferred_element_type=F32)
    if sigmoid_bias:
        acc = jax.nn.sigmoid(acc + b_ref[...])
    o_ref[...] = acc.astype(o_ref.dtype)


def _norm_matmul(x, g, w, tn, out_dtype, bias=None, tm=1024):
    t, d = x.shape
    ncols = w.shape[1]
    tm = min(tm, t)
    assert t % tm == 0 and ncols % tn == 0
    in_specs = [
        pl.BlockSpec((tm, d), lambda i, j: (i, 0)),
        pl.BlockSpec((1, d), lambda i, j: (0, 0)),
        pl.BlockSpec((d, tn), lambda i, j: (0, j)),
    ]
    args = [x, g.reshape(1, d), w]
    if bias is not None:
        in_specs.append(pl.BlockSpec((1, tn), lambda i, j: (0, j)))
        args.append(bias.reshape(1, ncols))
    return pl.pallas_call(
        functools.partial(_norm_matmul_kernel, sigmoid_bias=bias is not None),
        grid=(t // tm, ncols // tn),
        in_specs=in_specs,
        out_specs=pl.BlockSpec((tm, tn), lambda i, j: (i, j)),
        out_shape=jax.ShapeDtypeStruct((t, ncols), out_dtype),
        scratch_shapes=[pltpu.VMEM((tm, d), BF16)],
        compiler_params=_params("arbitrary", "arbitrary"),
        name="norm_matmul",
    )(*args)


def _hgrn_levels(c):
    ms = []
    m = 1
    while m <= c:
        ms.append(m)
        m *= 2
    return ms


def _hgrn_constants(c):
    t = np.arange(c)[:, None]
    r = np.arange(c)[None, :]
    sums = {"f": [], "b": []}
    masks = {"f": [np.eye(c)], "b": [np.eye(c)]}
    for m in _hgrn_levels(c):
        same = (t // m) == (r // m)
        if m > 1:
            sums["f"] += [same & (r <= t), same & (r > t)]
            sums["b"] += [same & (r >= t), same & (r < t)]
        if m < c:
            blk = (t // (2 * m)) == (r // (2 * m))
            t_hi, r_hi = (t // m) % 2 == 1, (r // m) % 2 == 1
            masks["f"].append(blk & t_hi & ~r_hi)
            masks["b"].append(blk & ~t_hi & r_hi)
    out = []
    for d in ("f", "b"):
        out.append(jnp.asarray(np.concatenate(sums[d], axis=0).astype(np.float32), BF16))
        out.append(jnp.asarray(np.stack(masks[d]).astype(np.float32), F32))
    return out


def _split2(x):
    hi = x.astype(BF16)
    lo = (x - hi.astype(F32)).astype(BF16)
    return hi, lo


def _hgrn_kernel(qf_ref, vf_ref, zf_ref, qb_ref, vb_ref, zb_ref, lbf_ref, lbb_ref,
                 sf_ref, mf_ref, sb_ref, mb_ref, of_ref, ob_ref, stf_ref, stb_ref, *, c, tb):
    @pl.when(pl.program_id(2) == 0)
    def _():
        stf_ref[...] = jnp.zeros_like(stf_ref)
        stb_ref[...] = jnp.zeros_like(stb_ref)

    def lower_bound(ref):
        lg = ref[...]
        ex = jnp.exp(lg - jnp.max(lg, axis=0, keepdims=True))
        return ex[0:1, :] / jnp.sum(ex, axis=0, keepdims=True)

    n = tb // c
    nlev = len(_hgrn_levels(c))
    items = []
    for ci in range(n):
        items.append((qf_ref, vf_ref, zf_ref, of_ref, sf_ref, mf_ref, "f", slice(ci * c, (ci + 1) * c), c - 1))
        items.append((qb_ref, vb_ref, zb_ref, ob_ref, sb_ref, mb_ref, "b", slice((n - 1 - ci) * c, (n - ci) * c), 0))
    lb = {"f": lower_bound(lbf_ref), "b": lower_bound(lbb_ref)}

    q, k, f, vb, kb = [], [], [], [], []
    for q_ref, v_ref, z_ref, _, _, _, d, sl, _ in items:
        fi = lb[d] + (1.0 - lb[d]) * jax.nn.sigmoid(z_ref[sl, :])
        f.append(fi)
        k.append(1.0 - fi)
        q.append(q_ref[sl, :])
        vb.append(v_ref[sl, :].astype(BF16))
        kb.append(k[-1].astype(BF16))
    e = []
    for it, fi in zip(items, f):
        cs2 = jnp.dot(it[4][...], jnp.concatenate(_split2(jnp.log(fi)), axis=1), preferred_element_type=F32)
        e.append(jnp.exp(cs2[:, :LANE] + cs2[:, LANE:]))
    a = [it[5][0] * lax.dot_general(qi.astype(BF16), kbi, NT_DIMS, preferred_element_type=F32)
         for it, qi, kbi in zip(items, q, kb)]
    for li in range(nlev - 1):
        for x, it in enumerate(items):
            if li == 0:
                qd, kd = (q[x] * f[x]).astype(BF16), kb[x]
            else:
                qd = (q[x] * e[x][2 * (li - 1) * c:(2 * li - 1) * c]).astype(BF16)
                kd = (k[x] * e[x][(2 * li - 1) * c:2 * li * c]).astype(BF16)
            a[x] = a[x] + it[5][li + 1] * lax.dot_general(qd, kd, NT_DIMS, preferred_element_type=F32)
    o = [jnp.dot(ai.astype(BF16), vbi, preferred_element_type=F32) for ai, vbi in zip(a, vb)]
    li = nlev - 1
    qd = [(q[x] * e[x][2 * (li - 1) * c:(2 * li - 1) * c]).astype(BF16) for x in range(len(items))]
    kv = [lax.dot_general(vb[x], (k[x] * e[x][(2 * li - 1) * c:2 * li * c]).astype(BF16), TN_DIMS,
                          preferred_element_type=F32) for x in range(len(items))]
    st = {"f": stf_ref[...], "b": stb_ref[...]}
    for x, it in enumerate(items):
        d, sl, total_row = it[6], it[7], it[8]
        it[3][sl, :] = o[x] + lax.dot_general(qd[x], st[d].astype(BF16), NT_DIMS, preferred_element_type=F32)
        decay = e[x][2 * (li - 1) * c + total_row:2 * (li - 1) * c + total_row + 1, :]
        st[d] = st[d] * decay + kv[x]
    stf_ref[...] = st["f"]
    stb_ref[...] = st["b"]


def _hgrn_scan(proj, lbf_logits, lbb_logits, batch, seq, tb=512, c=HG_CHUNK):
    t = batch * seq
    tb = min(tb, seq)
    nb = seq // tb
    assert seq % tb == 0 and tb % c == 0
    hh = HG_HEADS
    sums_f, masks_f, sums_b, masks_b = _hgrn_constants(c)
    nrow = lbf_logits.shape[0]

    def fwd(col):
        return pl.BlockSpec((tb, HG_DK), lambda b, h, i: (b * nb + i, col * hh + h))

    def bwd(col):
        return pl.BlockSpec((tb, HG_DK), lambda b, h, i: (b * nb + nb - 1 - i, col * hh + h))

    def const(a):
        return pl.BlockSpec(a.shape, lambda b, h, i: (0,) * a.ndim)

    lb_spec = pl.BlockSpec((nrow, HG_DK), lambda b, h, i: (0, h))
    out_sds = jax.ShapeDtypeStruct((t, HG_WIDTH), F32)
    return pl.pallas_call(
        functools.partial(_hgrn_kernel, c=c, tb=tb),
        grid=(batch, hh, nb),
        in_specs=[fwd(0), fwd(1), fwd(2), bwd(0), bwd(1), bwd(3), lb_spec, lb_spec,
                  const(sums_f), const(masks_f), const(sums_b), const(masks_b)],
        out_specs=[pl.BlockSpec((tb, HG_DK), lambda b, h, i: (b * nb + i, h)),
                   pl.BlockSpec((tb, HG_DK), lambda b, h, i: (b * nb + nb - 1 - i, h))],
        out_shape=[out_sds, out_sds],
        scratch_shapes=[pltpu.VMEM((HG_DK, HG_DK), F32), pltpu.VMEM((HG_DK, HG_DK), F32)],
        compiler_params=_params("arbitrary", "arbitrary", "arbitrary"),
        name="hgrn_scan",
    )(proj, proj, proj, proj, proj, proj, lbf_logits, lbb_logits, sums_f, masks_f, sums_b, masks_b)


def _dil_kernel(q_ref, k_ref, v_ref, o_ref, lse_ref, *, lq, length, win, n_side, slopes):
    q0 = pl.program_id(2) * lq
    start = pl.multiple_of(jnp.clip(q0 - n_side, 0, length - win), n_side)
    q = q_ref[...]
    kw = k_ref[pl.ds(start, win), :]
    vw = v_ref[pl.ds(start, win), :]
    qpos = q0 + lax.broadcasted_iota(jnp.int32, (lq, win), 0)
    kpos = start + lax.broadcasted_iota(jnp.int32, (lq, win), 1)
    dist = jnp.abs(kpos - qpos)
    valid = dist <= n_side
    distf = dist.astype(F32)
    outs, lses = [], []
    for i, slope in enumerate(slopes):
        cs = slice(i * DIL_HD, (i + 1) * DIL_HD)
        s = lax.dot_general(q[:, cs], kw[:, cs], NT_DIMS, preferred_element_type=F32)
        s = s * (DIL_HD ** -0.5) - slope * distf
        s = jnp.where(valid, s, -1e30)
        m = jnp.max(s, axis=-1, keepdims=True)
        p = jnp.exp(s - m)
        l = jnp.sum(p, axis=-1, keepdims=True)
        o = jnp.dot(p.astype(BF16), vw[:, cs], preferred_element_type=F32) / l
        outs.append(o)
        lses.append(jnp.broadcast_to(m + jnp.log(l), (lq, DIL_HD)))
    o_ref[...] = jnp.concatenate(outs, axis=1)
    lse_ref[...] = jnp.concatenate(lses, axis=1)


def _dilated_group(proj_at, gi, batch, seq):
    window, r = DIL_GROUPS[gi]
    n_side = (window // 2) // r
    t = batch * seq
    length = seq // r
    lq = min(256, length)
    win = min(length, lq + 2 * n_side)
    nq = length // lq
    assert seq % r == 0 and length % lq == 0 and lq % n_side == 0 and (length - win) % n_side == 0
    nblk = AT_COLS // DIL_GW
    assert AT_COLS % DIL_GW == 0
    ngrp = len(DIL_GROUPS)
    heads = np.arange(gi * DIL_HPG, (gi + 1) * DIL_HPG)
    slopes = tuple(float(s) * r for s in np.exp2(-ALIBI_MAX * (heads + 1) / DIL_HEADS))
    x = proj_at.reshape(t // r, r * AT_COLS)

    def kv_spec(which):
        return pl.BlockSpec((length, DIL_GW), lambda b, rho, qi: (b, rho * nblk + which * ngrp + gi))

    out_spec = pl.BlockSpec((lq, DIL_GW), lambda b, rho, qi: (b * nq + qi, rho))
    out_sds = jax.ShapeDtypeStruct((t // r, r * DIL_GW), F32)
    o, lse = pl.pallas_call(
        functools.partial(_dil_kernel, lq=lq, length=length, win=win, n_side=n_side, slopes=slopes),
        grid=(batch, r, nq),
        in_specs=[pl.BlockSpec((lq, DIL_GW), lambda b, rho, qi: (b * nq + qi, rho * nblk + gi)),
                  kv_spec(1), kv_spec(2)],
        out_specs=[out_spec, out_spec],
        out_shape=[out_sds, out_sds],
        compiler_params=_params("arbitrary", "arbitrary", "arbitrary"),
        name=f"dilated_attn_{gi}",
    )(x, x, x)
    return o.reshape(t, DIL_GW), lse.reshape(t, DIL_GW)


def _mem_attn_kernel(q_ref, k_ref, v_ref, o_ref):
    q, k, v = q_ref[...], k_ref[...], v_ref[...]
    outs = []
    for i in range(q.shape[1] // MEM_HD):
        cs = slice(i * MEM_HD, (i + 1) * MEM_HD)
        s = lax.dot_general(q[:, cs], k[:, cs], NT_DIMS, preferred_element_type=F32) * (MEM_HD ** -0.5)
        p = jnp.exp(s - jnp.max(s, axis=-1, keepdims=True))
        l = jnp.sum(p, axis=-1, keepdims=True)
        outs.append(jnp.dot(p.astype(BF16), v[:, cs], preferred_element_type=F32) / l)
    o_ref[...] = jnp.concatenate(outs, axis=1).astype(o_ref.dtype)


def _mem_attn(proj_at, kv, batch, seq, tq=1024):
    t = batch * seq
    tq = min(tq, seq)
    nq = seq // tq
    m = kv.shape[0] // batch
    bw = 2 * MEM_HD
    npair = MEM_WIDTH // bw
    q_off = (3 * DIL_WIDTH) // bw
    assert (3 * DIL_WIDTH) % bw == 0 and seq % tq == 0
    return pl.pallas_call(
        _mem_attn_kernel,
        grid=(batch, nq, npair),
        in_specs=[pl.BlockSpec((tq, bw), lambda b, i, hp: (b * nq + i, q_off + hp)),
                  pl.BlockSpec((m, bw), lambda b, i, hp: (b, hp)),
                  pl.BlockSpec((m, bw), lambda b, i, hp: (b, npair + hp))],
        out_specs=pl.BlockSpec((tq, bw), lambda b, i, hp: (b * nq + i, hp)),
        out_shape=jax.ShapeDtypeStruct((t, MEM_WIDTH), BF16),
        compiler_params=_params("arbitrary", "arbitrary", "arbitrary"),
        name="mem_attn",
    )(proj_at, kv, kv)


def _merge_kernel(of_ref, ob_ref, hg_ref, ghg_ref, od0_ref, od1_ref, od2_ref, l0_ref, l1_ref, l2_ref,
                  om_ref, gate_ref, wup_ref, wout_ref, x_ref, gffn_ref, x1_ref, h2_ref):
    o = of_ref[...] + ob_ref[...]
    ghg = ghg_ref[...]
    parts = [_rms(o[:, h * HG_DK:(h + 1) * HG_DK], ghg) for h in range(HG_HEADS)]
    gz = hg_ref[...]
    o_hg = (jnp.concatenate(parts, axis=1) * (gz * jax.nn.sigmoid(gz))).astype(BF16)

    l0, l1, l2 = l0_ref[...], l1_ref[...], l2_ref[...]
    mx = jnp.maximum(jnp.maximum(l0, l1), l2)
    e0, e1, e2 = jnp.exp(l0 - mx), jnp.exp(l1 - mx), jnp.exp(l2 - mx)
    inv = 1.0 / (e0 + e1 + e2)
    o_dil = jnp.concatenate([od0_ref[...] * (e0 * inv), od1_ref[...] * (e1 * inv), od2_ref[...] * (e2 * inv)],
                            axis=1).astype(BF16)

    d = x_ref.shape[1]
    y_hg = jnp.dot(o_hg, wup_ref[0:HG_WIDTH, :], preferred_element_type=F32)
    merged = gate_ref[:, 0:d].astype(F32) * y_hg
    y_dil = jnp.dot(o_dil, wup_ref[HG_WIDTH:HG_WIDTH + DIL_WIDTH, :], preferred_element_type=F32)
    merged = merged + gate_ref[:, d:2 * d].astype(F32) * y_dil
    y_mem = jnp.dot(om_ref[...], wup_ref[HG_WIDTH + DIL_WIDTH:, :], preferred_element_type=F32)
    merged = merged + gate_ref[:, 2 * d:].astype(F32) * y_mem
    x1 = x_ref[...] + jnp.dot(merged.astype(BF16), wout_ref[...], preferred_element_type=F32)
    x1_ref[...] = x1
    h2_ref[...] = _rms(x1, gffn_ref[...]).astype(h2_ref.dtype)


def _merge(o_f, o_b, proj_hg, g_hg, dil, o_mem, gates, w_up, w_out, x, g_ffn, tm=256):
    t, d = x.shape
    tm = min(tm, t)
    assert t % tm == 0
    row = lambda w, col=0: pl.BlockSpec((tm, w), lambda i: (i, col))
    const = lambda a: pl.BlockSpec(a.shape, lambda i: (0, 0), pipeline_mode=pl.Buffered(1))
    (od0, l0), (od1, l1), (od2, l2) = dil
    g_hg2, g_ffn2 = g_hg.reshape(1, HG_DK), g_ffn.reshape(1, d)
    return pl.pallas_call(
        _merge_kernel,
        grid=(t // tm,),
        in_specs=[row(HG_WIDTH), row(HG_WIDTH), row(HG_WIDTH, 4), const(g_hg2),
                  row(DIL_GW), row(DIL_GW), row(DIL_GW), row(DIL_GW), row(DIL_GW), row(DIL_GW),
                  row(MEM_WIDTH), row(GATE_COLS), const(w_up), const(w_out), row(d), const(g_ffn2)],
        out_specs=[row(d), row(d)],
        out_shape=[jax.ShapeDtypeStruct((t, d), F32), jax.ShapeDtypeStruct((t, d), BF16)],
        compiler_params=_params("arbitrary"),
        name="merge",
    )(o_f, o_b, proj_hg, g_hg2, od0, od1, od2, l0, l1, l2, o_mem, gates, w_up, w_out, x, g_ffn2)


N_TOP = PEER_TOPK + 1
TOP_ROWS = 24


def _top_values(s):
    rid = lax.broadcasted_iota(jnp.int32, (TOP_ROWS, s.shape[1]), 0)
    top = jnp.full((TOP_ROWS, s.shape[1]), -jnp.inf, F32)
    rows = []
    for r in range(N_TOP):
        m = jnp.max(s, axis=0, keepdims=True)
        rows.append(m)
        top = jnp.where(rid == r, m, top)
        s = jnp.where(s == m, -jnp.inf, s)
    return top, rows


def _peer_topk_kernel(h2_ref, wq_ref, keys_ref, thr_ref, w1_ref, s2_ref, e2_ref):
    h2 = h2_ref[...]
    qd = 2 * PEER_HALF
    for h in range(PEER_HEADS):
        q_t = lax.dot_general(wq_ref[h * qd:(h + 1) * qd, :], h2, NT_DIMS, preferred_element_type=F32)
        s1 = jnp.dot(keys_ref[0], q_t[:PEER_HALF].astype(BF16), preferred_element_type=F32)
        s2 = jnp.dot(keys_ref[1], q_t[PEER_HALF:].astype(BF16), preferred_element_type=F32)
        top1, rows1 = _top_values(s1)
        top2, rows2 = _top_values(s2)
        cands = [rows1[0] + top2]
        for a in range(1, 8):
            cands.append(rows1[a] + top2[0:8])
        cands.append(top1[8:TOP_ROWS] + rows2[0])
        cand = jnp.concatenate(cands, axis=0)
        best = []
        for r in range(N_TOP):
            m = jnp.max(cand, axis=0, keepdims=True)
            best.append(m)
            cand = jnp.where(cand == m, -jnp.inf, cand)
        z = jnp.ones_like(best[0])
        for r in range(1, PEER_TOPK):
            z = z + jnp.exp(best[r] - best[0])
        tau = 0.5 * (best[PEER_TOPK - 1] + best[PEER_TOPK])
        outs = ((thr_ref, tau - s1),
                (w1_ref, jnp.exp(s1 - rows1[0]) / z),
                (s2_ref, s2),
                (e2_ref, jnp.exp(s2 - rows2[0])))
        for ref, val in outs:
            for lt in range(ref.shape[0]):
                ref[lt, h] = val[:, lt * LANE:(lt + 1) * LANE]


def _peer_topk(h2, wq_t, keys, tb=256):
    t, d = h2.shape
    tb = min(tb, t)
    assert t % tb == 0 and tb % LANE == 0
    spec = pl.BlockSpec((tb // LANE, PEER_HEADS, PEER_NKEYS, LANE), lambda i: (i, 0, 0, 0))
    sds = jax.ShapeDtypeStruct((t // LANE, PEER_HEADS, PEER_NKEYS, LANE), F32)
    return pl.pallas_call(
        _peer_topk_kernel,
        grid=(t // tb,),
        in_specs=[pl.BlockSpec((tb, d), lambda i: (i, 0)),
                  pl.BlockSpec(wq_t.shape, lambda i: (0, 0)),
                  pl.BlockSpec(keys.shape, lambda i: (0, 0, 0))],
        out_specs=[spec] * 4,
        out_shape=[sds] * 4,
        compiler_params=_params("arbitrary"),
        name="peer_topk",
    )(h2, wq_t, keys)


PEER_ROWS = 16


def _peer_ffn_kernel(h2_ref, u_ref, vt_ref, thr_ref, w1_ref, s2_ref, e2_ref, o_ref, a_ref, hid_ref, *, eb):
    @pl.when(pl.program_id(1) == 0)
    def _():
        o_ref[...] = jnp.zeros_like(o_ref)

    tb = h2_ref.shape[0]
    a_ref[...] = lax.dot_general(u_ref[...], h2_ref[...], NT_DIMS, preferred_element_type=F32)

    def body(ii, carry):
        r0 = pl.multiple_of(ii * PEER_NKEYS, PEER_NKEYS)
        for lt in range(tb // LANE):
            ls = slice(lt * LANE, (lt + 1) * LANE)
            thr = [jnp.broadcast_to(thr_ref[lt, h, pl.ds(ii, 1), :], (PEER_ROWS, LANE)) for h in range(PEER_HEADS)]
            w1 = [jnp.broadcast_to(w1_ref[lt, h, pl.ds(ii, 1), :], (PEER_ROWS, LANE)) for h in range(PEER_HEADS)]
            for rc in range(PEER_NKEYS // PEER_ROWS):
                js = slice(rc * PEER_ROWS, (rc + 1) * PEER_ROWS)
                g = None
                for h in range(PEER_HEADS):
                    term = jnp.where(s2_ref[lt, h, js, :] >= thr[h], e2_ref[lt, h, js, :] * w1[h], 0.0)
                    g = term if g is None else g + term
                a = a_ref[pl.ds(r0 + rc * PEER_ROWS, PEER_ROWS), ls]
                gelu = 0.5 * a * (1.0 + lax.erf(a * (0.5 ** 0.5)))
                hid_ref[pl.ds(r0 + rc * PEER_ROWS, PEER_ROWS), ls] = (gelu * g).astype(hid_ref.dtype)
        return carry

    lax.fori_loop(0, eb // PEER_NKEYS, body, 0)
    o_ref[...] += jnp.dot(vt_ref[...], hid_ref[...], preferred_element_type=F32)


def _peer_ffn(h2, u, v_t, thr, w1, s2, e2, tb=512, eb=1024):
    t, d = h2.shape
    ne = u.shape[0]
    tb, eb = min(tb, t), min(eb, ne)
    assert t % tb == 0 and tb % LANE == 0 and ne % eb == 0 and eb % PEER_NKEYS == 0
    ni = eb // PEER_NKEYS
    nlt = tb // LANE
    tok = pl.BlockSpec((nlt, PEER_HEADS, PEER_NKEYS, LANE), lambda i, e: (i, 0, 0, 0))
    sel = pl.BlockSpec((nlt, PEER_HEADS, ni, LANE), lambda i, e: (i, 0, e, 0))
    return pl.pallas_call(
        functools.partial(_peer_ffn_kernel, eb=eb),
        grid=(t // tb, ne // eb),
        in_specs=[pl.BlockSpec((tb, d), lambda i, e: (i, 0)),
                  pl.BlockSpec((eb, d), lambda i, e: (e, 0)),
                  pl.BlockSpec((d, eb), lambda i, e: (0, e)),
                  sel, sel, tok, tok],
        out_specs=pl.BlockSpec((d, tb), lambda i, e: (0, i)),
        out_shape=jax.ShapeDtypeStruct((d, t), F32),
        scratch_shapes=[pltpu.VMEM((eb, tb), F32), pltpu.VMEM((eb, tb), BF16)],
        compiler_params=_params("arbitrary", "arbitrary"),
        name="peer_ffn",
    )(h2, u, v_t, thr, w1, s2, e2)


def _final_kernel(x1_ref, pt_ref, g_ref, o_ref):
    o_ref[...] = _rms(x1_ref[...] + pt_ref[...].T, g_ref[...])


def _final(x1, peer_t, g, tb=512):
    t, d = x1.shape
    tb = min(tb, t)
    assert t % tb == 0
    return pl.pallas_call(
        _final_kernel,
        grid=(t // tb,),
        in_specs=[pl.BlockSpec((tb, d), lambda i: (i, 0)),
                  pl.BlockSpec((d, tb), lambda i: (0, i)),
                  pl.BlockSpec((1, d), lambda i: (0, 0))],
        out_specs=pl.BlockSpec((tb, d), lambda i: (i, 0)),
        out_shape=jax.ShapeDtypeStruct((t, d), F32),
        compiler_params=_params("arbitrary"),
        name="final_norm",
    )(x1, peer_t, g.reshape(1, d))


def kernel(x, mem, g_mix, w_in, b_gate, lb_fwd_logits, lb_bwd_logits, g_hgrn, w_mem_kv, g_mem, w_up, w_out,
           g_ffn, w_peer_q, peer_sub_keys, peer_u, peer_v, g_final):
    batch, seq, d = x.shape
    assert d == D_MODEL and w_in.shape[0] == 1, "one layer"
    t = batch * seq
    xt = x.reshape(t, d)
    w_hg = w_in[0, :, :HG_COLS].astype(BF16)
    w_at = w_in[0, :, HG_COLS:HG_COLS + AT_COLS].astype(BF16)
    w_gate = w_in[0, :, HG_COLS + AT_COLS:].astype(BF16)

    proj_hg = _norm_matmul(xt, g_mix[0], w_hg, 768, F32)
    proj_at = _norm_matmul(xt, g_mix[0], w_at, 1408, BF16)
    gates = _norm_matmul(xt, g_mix[0], w_gate, 1024, BF16, bias=b_gate[0])

    o_f, o_b = _hgrn_scan(proj_hg, lb_fwd_logits, lb_bwd_logits, batch, seq)
    dil = [_dilated_group(proj_at, gi, batch, seq) for gi in range(len(DIL_GROUPS))]
    kv = _norm_matmul(mem.reshape(-1, d), g_mem[0], w_mem_kv[0].astype(BF16), 2 * MEM_WIDTH, BF16)
    o_mem = _mem_attn(proj_at, kv, batch, seq)

    x1, h2 = _merge(o_f, o_b, proj_hg, g_hgrn[0], dil, o_mem, gates, w_up[0].astype(BF16), w_out[0].astype(BF16),
                    xt, g_ffn[0])

    thr, w1, s2, e2 = _peer_topk(h2, w_peer_q[0].T.astype(BF16), peer_sub_keys[0].astype(BF16))
    peer_t = _peer_ffn(h2, peer_u[0].astype(BF16), peer_v[0].T.astype(BF16), thr, w1, s2, e2)
    return _final(x1, peer_t, g_final).reshape(batch, seq, d)
```

```python
import functools

import numpy as np
import jax
import jax.numpy as jnp
from jax import lax
from jax.experimental import pallas as pl
from jax.experimental.pallas import tpu as pltpu

F32 = jnp.float32
BF16 = jnp.bfloat16

EPS = 1e-6
LANE = 128
VMEM_LIMIT = 56 * 1024 * 1024

D_MODEL = 2048
HG_HEADS, HG_DK = 6, 128
HG_WIDTH = HG_HEADS * HG_DK
HG_CHUNK = 64
DIL_GROUPS = ((128, 1), (512, 4), (2048, 16))
DIL_HPG, DIL_HD = 4, 64
DIL_GW = DIL_HPG * DIL_HD
DIL_WIDTH = DIL_GW * len(DIL_GROUPS)
DIL_HEADS = DIL_HPG * len(DIL_GROUPS)
ALIBI_MAX = 8.0
MEM_HEADS, MEM_HD = 4, 128
MEM_WIDTH = MEM_HEADS * MEM_HD
N_BRANCH = 3
HG_COLS = 5 * HG_WIDTH
AT_COLS = 3 * DIL_WIDTH + MEM_WIDTH
GATE_COLS = N_BRANCH * D_MODEL
PEER_HEADS, PEER_NKEYS, PEER_HALF, PEER_TOPK = 8, 128, 128, 16

NT_DIMS = (((1,), (1,)), ((), ()))
TN_DIMS = (((0,), (0,)), ((), ()))


def _params(*sem):
    return pltpu.CompilerParams(dimension_semantics=sem, vmem_limit_bytes=VMEM_LIMIT)


def _rms(x, g):
    return x * lax.rsqrt(jnp.mean(x * x, axis=-1, keepdims=True) + EPS) * g


def _normed_input(x_ref, g_ref, h_ref):
    @pl.when(pl.program_id(1) == 0)
    def _():
        h_ref[...] = _rms(x_ref[...], g_ref[...]).astype(h_ref.dtype)

    return h_ref[...]


def _proj_kernel(x_ref, g_ref, *rest, nw, sigmoid_bias):
    w_refs, rest = rest[:nw], rest[nw:]
    b_refs, rest = (rest[:nw], rest[nw:]) if sigmoid_bias else ((), rest)
    o_ref, h_ref = rest
    h = _normed_input(x_ref, g_ref, h_ref)
    wb = w_refs[0].shape[1]
    for k in range(nw):
        acc = jnp.dot(h, w_refs[k][...].astype(BF16), preferred_element_type=F32)
        if sigmoid_bias:
            acc = jax.nn.sigmoid(acc + b_refs[k][...])
        o_ref[:, k * wb:(k + 1) * wb] = acc.astype(o_ref.dtype)


def _proj(x, g, w, col0, ncols, wb, nw, out_dtype, bias=None, tm=1024):
    t, d = x.shape
    tm = min(tm, t)
    tn = wb * nw
    assert t % tm == 0 and ncols % tn == 0 and col0 % wb == 0
    off = col0 // wb
    in_specs = [pl.BlockSpec((tm, d), lambda i, j: (i, 0)), pl.BlockSpec((1, d), lambda i, j: (0, 0))]
    in_specs += [pl.BlockSpec((d, wb), lambda i, j, k=k: (0, off + nw * j + k)) for k in range(nw)]
    args = [x, g.reshape(1, d)] + [w] * nw
    if bias is not None:
        in_specs += [pl.BlockSpec((1, wb), lambda i, j, k=k: (0, nw * j + k)) for k in range(nw)]
        args += [bias.reshape(1, ncols)] * nw
    return pl.pallas_call(
        functools.partial(_proj_kernel, nw=nw, sigmoid_bias=bias is not None),
        grid=(t // tm, ncols // tn),
        in_specs=in_specs,
        out_specs=pl.BlockSpec((tm, tn), lambda i, j: (i, j)),
        out_shape=jax.ShapeDtypeStruct((t, ncols), out_dtype),
        scratch_shapes=[pltpu.VMEM((tm, d), BF16)],
        compiler_params=_params("arbitrary", "arbitrary"),
        name="proj",
    )(*args)


def _attn_proj_kernel(x_ref, g_ref, wq_ref, wk_ref, wv_ref, *rest, dils):
    out_refs, mq_ref, h_ref, scr_ref = rest[:len(dils)], rest[len(dils)], rest[-2], rest[-1]
    j = pl.program_id(1)
    h = _normed_input(x_ref, g_ref, h_ref)
    acc = [jnp.dot(h, w[...].astype(BF16), preferred_element_type=F32) for w in (wq_ref, wk_ref, wv_ref)]
    tm = x_ref.shape[0]
    for gi, r in enumerate(dils):
        @pl.when(j == gi)
        def _(gi=gi, r=r):
            o_ref = out_refs[gi]
            if r == 1:
                for k in range(3):
                    o_ref[0, :, k * DIL_GW:(k + 1) * DIL_GW] = acc[k].astype(o_ref.dtype)
                return
            for k in range(3):
                for c in range(DIL_GW // LANE):
                    scr_ref[(DIL_GW // LANE) * k + c] = acc[k][:, c * LANE:(c + 1) * LANE]
            for rho in range(r):
                for c in range(3 * DIL_GW // LANE):
                    rows = scr_ref[c, pl.ds(rho, tm // r, stride=r), :]
                    o_ref[rho, :, c * LANE:(c + 1) * LANE] = rows.astype(o_ref.dtype)

    @pl.when(j == len(dils))
    def _():
        for k in range(2):
            mq_ref[:, k * DIL_GW:(k + 1) * DIL_GW] = acc[k].astype(mq_ref.dtype)


def _attn_proj(x, g, w, col0, batch, seq, tm=1024):
    t, d = x.shape
    tm = min(tm, seq)
    dils = tuple(r for _, r in DIL_GROUPS)
    ng = len(dils)
    assert seq % tm == 0 and col0 % DIL_GW == 0 and MEM_WIDTH == 2 * DIL_GW and all(tm % (16 * r) == 0 for r in dils)
    off = col0 // DIL_GW
    nbs = seq // tm

    def w_spec(which):
        return pl.BlockSpec((d, DIL_GW), lambda i, j: (0, off + jnp.where(
            j < ng, which * ng + j, 3 * ng + jnp.minimum(which, 1))))

    out_specs = [pl.BlockSpec((None, r, tm // r, 3 * DIL_GW), lambda i, j: (i // nbs, 0, i % nbs, 0)) for r in dils]
    out_shape = [jax.ShapeDtypeStruct((batch, r, seq // r, 3 * DIL_GW), BF16) for r in dils]
    out_specs.append(pl.BlockSpec((tm, MEM_WIDTH), lambda i, j: (i, 0)))
    out_shape.append(jax.ShapeDtypeStruct((t, MEM_WIDTH), BF16))
    outs = pl.pallas_call(
        functools.partial(_attn_proj_kernel, dils=dils),
        grid=(t // tm, ng + 1),
        in_specs=[pl.BlockSpec((tm, d), lambda i, j: (i, 0)), pl.BlockSpec((1, d), lambda i, j: (0, 0)),
                  w_spec(0), w_spec(1), w_spec(2)],
        out_specs=out_specs,
        out_shape=out_shape,
        scratch_shapes=[pltpu.VMEM((tm, d), BF16), pltpu.VMEM((3 * DIL_GW // LANE, tm, LANE), F32)],
        compiler_params=_params("arbitrary", "arbitrary"),
        name="attn_proj",
    )(x, g.reshape(1, d), w, w, w)
    return outs[:ng], outs[ng]


def _hgrn_levels(c):
    ms = []
    m = 1
    while m <= c:
        ms.append(m)
        m *= 2
    return ms


def _hgrn_constants(c):
    t = np.arange(c)[:, None]
    r = np.arange(c)[None, :]
    sums = {"f": [], "b": []}
    masks = {"f": [np.eye(c)], "b": [np.eye(c)]}
    for m in _hgrn_levels(c):
        same = (t // m) == (r // m)
        if m > 1:
            sums["f"] += [same & (r <= t), same & (r > t)]
            sums["b"] += [same & (r >= t), same & (r < t)]
        if m < c:
            blk = (t // (2 * m)) == (r // (2 * m))
            t_hi, r_hi = (t // m) % 2 == 1, (r // m) % 2 == 1
            masks["f"].append(blk & t_hi & ~r_hi)
            masks["b"].append(blk & ~t_hi & r_hi)
    out = []
    for d in ("f", "b"):
        out.append(jnp.asarray(np.concatenate(sums[d], axis=0).astype(np.float32), BF16))
        out.append(jnp.asarray(np.stack(masks[d]).astype(np.float32), F32))
    return out


def _split2(x):
    hi = x.astype(BF16)
    lo = (x - hi.astype(F32)).astype(BF16)
    return hi, lo


def _hgrn_kernel(qf_ref, vf_ref, zf_ref, qb_ref, vb_ref, zb_ref, lbf_ref, lbb_ref,
                 sf_ref, mf_ref, sb_ref, mb_ref, of_ref, ob_ref, stf_ref, stb_ref, *, c, tb):
    @pl.when(pl.program_id(2) == 0)
    def _():
        stf_ref[...] = jnp.zeros_like(stf_ref)
        stb_ref[...] = jnp.zeros_like(stb_ref)

    def lower_bound(ref):
        lg = ref[...]
        ex = jnp.exp(lg - jnp.max(lg, axis=0, keepdims=True))
        return ex[0:1, :] / jnp.sum(ex, axis=0, keepdims=True)

    n = tb // c
    nlev = len(_hgrn_levels(c))
    items = []
    for ci in range(n):
        items.append((qf_ref, vf_ref, zf_ref, of_ref, sf_ref, mf_ref, "f", slice(ci * c, (ci + 1) * c), c - 1))
        items.append((qb_ref, vb_ref, zb_ref, ob_ref, sb_ref, mb_ref, "b", slice((n - 1 - ci) * c, (n - ci) * c), 0))
    lb = {"f": lower_bound(lbf_ref), "b": lower_bound(lbb_ref)}

    q, k, f, vb, kb = [], [], [], [], []
    for q_ref, v_ref, z_ref, _, _, _, d, sl, _ in items:
        fi = lb[d] + (1.0 - lb[d]) * jax.nn.sigmoid(z_ref[sl, :])
        f.append(fi)
        k.append(1.0 - fi)
        q.append(q_ref[sl, :])
        vb.append(v_ref[sl, :].astype(BF16))
        kb.append(k[-1].astype(BF16))
    e = []
    for it, fi in zip(items, f):
        cs2 = jnp.dot(it[4][...], jnp.concatenate(_split2(jnp.log(fi)), axis=1), preferred_element_type=F32)
        e.append(jnp.exp(cs2[:, :LANE] + cs2[:, LANE:]))
    a = [it[5][0] * lax.dot_general(qi.astype(BF16), kbi, NT_DIMS, preferred_element_type=F32)
         for it, qi, kbi in zip(items, q, kb)]
    for li in range(nlev - 1):
        for x, it in enumerate(items):
            if li == 0:
                qd, kd = (q[x] * f[x]).astype(BF16), kb[x]
            else:
                qd = (q[x] * e[x][2 * (li - 1) * c:(2 * li - 1) * c]).astype(BF16)
                kd = (k[x] * e[x][(2 * li - 1) * c:2 * li * c]).astype(BF16)
            a[x] = a[x] + it[5][li + 1] * lax.dot_general(qd, kd, NT_DIMS, preferred_element_type=F32)
    o = [jnp.dot(ai.astype(BF16), vbi, preferred_element_type=F32) for ai, vbi in zip(a, vb)]
    li = nlev - 1
    qd = [(q[x] * e[x][2 * (li - 1) * c:(2 * li - 1) * c]).astype(BF16) for x in range(len(items))]
    kv = [lax.dot_general(vb[x], (k[x] * e[x][(2 * li - 1) * c:2 * li * c]).astype(BF16), TN_DIMS,
                          preferred_element_type=F32) for x in range(len(items))]
    st = {"f": stf_ref[...], "b": stb_ref[...]}
    for x, it in enumerate(items):
        d, sl, total_row = it[6], it[7], it[8]
        it[3][sl, :] = o[x] + lax.dot_general(qd[x], st[d].astype(BF16), NT_DIMS, preferred_element_type=F32)
        decay = e[x][2 * (li - 1) * c + total_row:2 * (li - 1) * c + total_row + 1, :]
        st[d] = st[d] * decay + kv[x]
    stf_ref[...] = st["f"]
    stb_ref[...] = st["b"]


def _hgrn_scan(proj, lbf_logits, lbb_logits, batch, seq, tb=512, c=HG_CHUNK):
    t = batch * seq
    tb = min(tb, seq)
    nb = seq // tb
    assert seq % tb == 0 and tb % c == 0
    hh = HG_HEADS
    sums_f, masks_f, sums_b, masks_b = _hgrn_constants(c)
    nrow = lbf_logits.shape[0]

    def fwd(col):
        return pl.BlockSpec((tb, HG_DK), lambda b, h, i: (b * nb + i, col * hh + h))

    def bwd(col):
        return pl.BlockSpec((tb, HG_DK), lambda b, h, i: (b * nb + nb - 1 - i, col * hh + h))

    def const(a):
        return pl.BlockSpec(a.shape, lambda b, h, i: (0,) * a.ndim)

    lb_spec = pl.BlockSpec((nrow, HG_DK), lambda b, h, i: (0, h))
    out_sds = jax.ShapeDtypeStruct((t, HG_WIDTH), F32)
    return pl.pallas_call(
        functools.partial(_hgrn_kernel, c=c, tb=tb),
        grid=(batch, hh, nb),
        in_specs=[fwd(0), fwd(1), fwd(2), bwd(0), bwd(1), bwd(3), lb_spec, lb_spec,
                  const(sums_f), const(masks_f), const(sums_b), const(masks_b)],
        out_specs=[pl.BlockSpec((tb, HG_DK), lambda b, h, i: (b * nb + i, h)),
                   pl.BlockSpec((tb, HG_DK), lambda b, h, i: (b * nb + nb - 1 - i, h))],
        out_shape=[out_sds, out_sds],
        scratch_shapes=[pltpu.VMEM((HG_DK, HG_DK), F32), pltpu.VMEM((HG_DK, HG_DK), F32)],
        compiler_params=_params("arbitrary", "arbitrary", "arbitrary"),
        name="hgrn_scan",
    )(proj, proj, proj, proj, proj, proj, lbf_logits, lbb_logits, sums_f, masks_f, sums_b, masks_b)


def _dil_kernel(q_ref, k_ref, v_ref, o_ref, lse_ref, *, lq, length, win, n_side, slopes):
    q0 = pl.program_id(2) * lq
    start = pl.multiple_of(jnp.clip(q0 - n_side, 0, length - win), n_side)
    q = q_ref[...]
    kw = k_ref[pl.ds(start, win), :]
    vw = v_ref[pl.ds(start, win), :]
    qpos = q0 + lax.broadcasted_iota(jnp.int32, (lq, win), 0)
    kpos = start + lax.broadcasted_iota(jnp.int32, (lq, win), 1)
    dist = jnp.abs(kpos - qpos)
    valid = dist <= n_side
    distf = dist.astype(F32)
    outs, lses = [], []
    for i, slope in enumerate(slopes):
        cs = slice(i * DIL_HD, (i + 1) * DIL_HD)
        s = lax.dot_general(q[:, cs], kw[:, cs], NT_DIMS, preferred_element_type=F32)
        s = s * (DIL_HD ** -0.5) - slope * distf
        s = jnp.where(valid, s, -1e30)
        m = jnp.max(s, axis=-1, keepdims=True)
        p = jnp.exp(s - m)
        l = jnp.sum(p, axis=-1, keepdims=True)
        o = jnp.dot(p.astype(BF16), vw[:, cs], preferred_element_type=F32) / l
        outs.append(o)
        lses.append(jnp.broadcast_to(m + jnp.log(l), (lq, DIL_HD)))
    o_ref[...] = jnp.concatenate(outs, axis=1)
    lse_ref[...] = jnp.concatenate(lses, axis=1)


def _dilated_group(qkv, gi, batch, seq):
    window, r = DIL_GROUPS[gi]
    n_side = (window // 2) // r
    t = batch * seq
    length = seq // r
    lq = min(256, length)
    win = min(length, lq + 2 * n_side)
    nq = length // lq
    assert seq % r == 0 and length % lq == 0 and lq % n_side == 0 and (length - win) % n_side == 0
    heads = np.arange(gi * DIL_HPG, (gi + 1) * DIL_HPG)
    slopes = tuple(float(s) * r for s in np.exp2(-ALIBI_MAX * (heads + 1) / DIL_HEADS))
    x = qkv.reshape(batch * r * length, 3 * DIL_GW)

    def kv_spec(which):
        return pl.BlockSpec((length, DIL_GW), lambda b, rho, qi: (b * r + rho, which))

    out_spec = pl.BlockSpec((lq, DIL_GW), lambda b, rho, qi: (b * nq + qi, rho))
    out_sds = jax.ShapeDtypeStruct((t // r, r * DIL_GW), F32)
    o, lse = pl.pallas_call(
        functools.partial(_dil_kernel, lq=lq, length=length, win=win, n_side=n_side, slopes=slopes),
        grid=(batch, r, nq),
        in_specs=[pl.BlockSpec((lq, DIL_GW), lambda b, rho, qi: ((b * r + rho) * nq + qi, 0)),
                  kv_spec(1), kv_spec(2)],
        out_specs=[out_spec, out_spec],
        out_shape=[out_sds, out_sds],
        compiler_params=_params("arbitrary", "arbitrary", "arbitrary"),
        name=f"dilated_attn_{gi}",
    )(x, x, x)
    return o.reshape(t, DIL_GW), lse.reshape(t, DIL_GW)


def _mem_attn_kernel(q_ref, k_ref, v_ref, o_ref):
    q, k, v = q_ref[...], k_ref[...], v_ref[...]
    outs = []
    for i in range(q.shape[1] // MEM_HD):
        cs = slice(i * MEM_HD, (i + 1) * MEM_HD)
        s = lax.dot_general(q[:, cs], k[:, cs], NT_DIMS, preferred_element_type=F32) * (MEM_HD ** -0.5)
        p = jnp.exp(s - jnp.max(s, axis=-1, keepdims=True))
        l = jnp.sum(p, axis=-1, keepdims=True)
        outs.append(jnp.dot(p.astype(BF16), v[:, cs], preferred_element_type=F32) / l)
    o_ref[...] = jnp.concatenate(outs, axis=1).astype(o_ref.dtype)


def _mem_attn(mq, kv, batch, seq, tq=1024):
    t = batch * seq
    tq = min(tq, seq)
    nq = seq // tq
    m = kv.shape[0] // batch
    bw = 2 * MEM_HD
    npair = MEM_WIDTH // bw
    assert seq % tq == 0
    return pl.pallas_call(
        _mem_attn_kernel,
        grid=(batch, nq, npair),
        in_specs=[pl.BlockSpec((tq, bw), lambda b, i, hp: (b * nq + i, hp)),
                  pl.BlockSpec((m, bw), lambda b, i, hp: (b, hp)),
                  pl.BlockSpec((m, bw), lambda b, i, hp: (b, npair + hp))],
        out_specs=pl.BlockSpec((tq, bw), lambda b, i, hp: (b * nq + i, hp)),
        out_shape=jax.ShapeDtypeStruct((t, MEM_WIDTH), BF16),
        compiler_params=_params("arbitrary", "arbitrary", "arbitrary"),
        name="mem_attn",
    )(mq, kv, kv)


def _merge_kernel(of_ref, ob_ref, hg_ref, ghg_ref, od0_ref, od1_ref, od2_ref, l0_ref, l1_ref, l2_ref,
                  om_ref, gate_ref, wup_ref, wout_ref, x_ref, gffn_ref, x1_ref, h2_ref):
    o = of_ref[...] + ob_ref[...]
    ghg = ghg_ref[...]
    parts = [_rms(o[:, h * HG_DK:(h + 1) * HG_DK], ghg) for h in range(HG_HEADS)]
    gz = hg_ref[...]
    o_hg = (jnp.concatenate(parts, axis=1) * (gz * jax.nn.sigmoid(gz))).astype(BF16)

    l0, l1, l2 = l0_ref[...], l1_ref[...], l2_ref[...]
    mx = jnp.maximum(jnp.maximum(l0, l1), l2)
    e0, e1, e2 = jnp.exp(l0 - mx), jnp.exp(l1 - mx), jnp.exp(l2 - mx)
    inv = 1.0 / (e0 + e1 + e2)
    o_dil = jnp.concatenate([od0_ref[...] * (e0 * inv), od1_ref[...] * (e1 * inv), od2_ref[...] * (e2 * inv)],
                            axis=1).astype(BF16)

    d = x_ref.shape[1]
    y_hg = jnp.dot(o_hg, wup_ref[0:HG_WIDTH, :], preferred_element_type=F32)
    merged = gate_ref[:, 0:d].astype(F32) * y_hg
    y_dil = jnp.dot(o_dil, wup_ref[HG_WIDTH:HG_WIDTH + DIL_WIDTH, :], preferred_element_type=F32)
    merged = merged + gate_ref[:, d:2 * d].astype(F32) * y_dil
    y_mem = jnp.dot(om_ref[...], wup_ref[HG_WIDTH + DIL_WIDTH:, :], preferred_element_type=F32)
    merged = merged + gate_ref[:, 2 * d:].astype(F32) * y_mem
    x1 = x_ref[...] + jnp.dot(merged.astype(BF16), wout_ref[...], preferred_element_type=F32)
    x1_ref[...] = x1
    h2_ref[...] = _rms(x1, gffn_ref[...]).astype(h2_ref.dtype)


def _merge(o_f, o_b, proj_hg, g_hg, dil, o_mem, gates, w_up, w_out, x, g_ffn, tm=256):
    t, d = x.shape
    tm = min(tm, t)
    assert t % tm == 0
    row = lambda w, col=0: pl.BlockSpec((tm, w), lambda i: (i, col))
    const = lambda a: pl.BlockSpec(a.shape, lambda i: (0, 0), pipeline_mode=pl.Buffered(1))
    (od0, l0), (od1, l1), (od2, l2) = dil
    g_hg2, g_ffn2 = g_hg.reshape(1, HG_DK), g_ffn.reshape(1, d)
    return pl.pallas_call(
        _merge_kernel,
        grid=(t // tm,),
        in_specs=[row(HG_WIDTH), row(HG_WIDTH), row(HG_WIDTH, 4), const(g_hg2),
                  row(DIL_GW), row(DIL_GW), row(DIL_GW), row(DIL_GW), row(DIL_GW), row(DIL_GW),
                  row(MEM_WIDTH), row(GATE_COLS), const(w_up), const(w_out), row(d), const(g_ffn2)],
        out_specs=[row(d), row(d)],
        out_shape=[jax.ShapeDtypeStruct((t, d), F32), jax.ShapeDtypeStruct((t, d), BF16)],
        compiler_params=_params("arbitrary"),
        name="merge",
    )(o_f, o_b, proj_hg, g_hg2, od0, od1, od2, l0, l1, l2, o_mem, gates, w_up, w_out, x, g_ffn2)


N_TOP = PEER_TOPK + 1
TOP_ROWS = 24


def _top_values(s):
    rid = lax.broadcasted_iota(jnp.int32, (TOP_ROWS, s.shape[1]), 0)
    top = jnp.full((TOP_ROWS, s.shape[1]), -jnp.inf, F32)
    rows = []
    for r in range(N_TOP):
        m = jnp.max(s, axis=0, keepdims=True)
        rows.append(m)
        top = jnp.where(rid == r, m, top)
        s = jnp.where(s == m, -jnp.inf, s)
    return top, rows


def _peer_topk_kernel(h2_ref, wq_ref, keys_ref, u_ref, v_ref, thr_ref, w1_ref, s2_ref, e2_ref, ub_ref, vt_ref):
    ub_ref[...] = u_ref[...].astype(ub_ref.dtype)
    vt_ref[...] = v_ref[...].T.astype(vt_ref.dtype)
    h2 = h2_ref[...]
    qd = 2 * PEER_HALF
    for h in range(PEER_HEADS):
        q_t = lax.dot_general(wq_ref[h * qd:(h + 1) * qd, :], h2, NT_DIMS, preferred_element_type=F32)
        s1 = jnp.dot(keys_ref[0], q_t[:PEER_HALF].astype(BF16), preferred_element_type=F32)
        s2 = jnp.dot(keys_ref[1], q_t[PEER_HALF:].astype(BF16), preferred_element_type=F32)
        top1, rows1 = _top_values(s1)
        top2, rows2 = _top_values(s2)
        cands = [rows1[0] + top2]
        for a in range(1, 8):
            cands.append(rows1[a] + top2[0:8])
        cands.append(top1[8:TOP_ROWS] + rows2[0])
        cand = jnp.concatenate(cands, axis=0)
        best = []
        for r in range(N_TOP):
            m = jnp.max(cand, axis=0, keepdims=True)
            best.append(m)
            cand = jnp.where(cand == m, -jnp.inf, cand)
        z = jnp.ones_like(best[0])
        for r in range(1, PEER_TOPK):
            z = z + jnp.exp(best[r] - best[0])
        tau = 0.5 * (best[PEER_TOPK - 1] + best[PEER_TOPK])
        outs = ((thr_ref, tau - s1),
                (w1_ref, jnp.exp(s1 - rows1[0]) / z),
                (s2_ref, s2),
                (e2_ref, jnp.exp(s2 - rows2[0])))
        for ref, val in outs:
            for lt in range(ref.shape[0]):
                ref[lt, h] = val[:, lt * LANE:(lt + 1) * LANE]


def _peer_topk(h2, wq_t, keys, u, v, tb=256):
    t, d = h2.shape
    ne = u.shape[0]
    tb = min(tb, t)
    steps = t // tb
    assert t % tb == 0 and tb % LANE == 0 and ne % steps == 0
    rows = ne // steps
    spec = pl.BlockSpec((tb // LANE, PEER_HEADS, PEER_NKEYS, LANE), lambda i: (i, 0, 0, 0))
    sds = jax.ShapeDtypeStruct((t // LANE, PEER_HEADS, PEER_NKEYS, LANE), F32)
    return pl.pallas_call(
        _peer_topk_kernel,
        grid=(steps,),
        in_specs=[pl.BlockSpec((tb, d), lambda i: (i, 0)),
                  pl.BlockSpec(wq_t.shape, lambda i: (0, 0), pipeline_mode=pl.Buffered(1)),
                  pl.BlockSpec(keys.shape, lambda i: (0, 0, 0)),
                  pl.BlockSpec((rows, d), lambda i: (i, 0)),
                  pl.BlockSpec((rows, d), lambda i: (i, 0))],
        out_specs=[spec] * 4 + [pl.BlockSpec((rows, d), lambda i: (i, 0)), pl.BlockSpec((d, rows), lambda i: (0, i))],
        out_shape=[sds] * 4 + [jax.ShapeDtypeStruct((ne, d), BF16), jax.ShapeDtypeStruct((d, ne), BF16)],
        compiler_params=_params("arbitrary"),
        name="peer_topk",
    )(h2, wq_t, keys, u, v)


PEER_ROWS = 16


def _peer_ffn_kernel(h2_ref, u_ref, vt_ref, thr_ref, w1_ref, s2_ref, e2_ref, o_ref, a_ref, hid_ref, *, eb):
    @pl.when(pl.program_id(1) == 0)
    def _():
        o_ref[...] = jnp.zeros_like(o_ref)

    tb = h2_ref.shape[0]
    a_ref[...] = lax.dot_general(u_ref[...], h2_ref[...], NT_DIMS, preferred_element_type=F32)

    def body(ii, carry):
        r0 = pl.multiple_of(ii * PEER_NKEYS, PEER_NKEYS)
        for lt in range(tb // LANE):
            ls = slice(lt * LANE, (lt + 1) * LANE)
            thr = [jnp.broadcast_to(thr_ref[lt, h, pl.ds(ii, 1), :], (PEER_ROWS, LANE)) for h in range(PEER_HEADS)]
            w1 = [jnp.broadcast_to(w1_ref[lt, h, pl.ds(ii, 1), :], (PEER_ROWS, LANE)) for h in range(PEER_HEADS)]
            for rc in range(PEER_NKEYS // PEER_ROWS):
                js = slice(rc * PEER_ROWS, (rc + 1) * PEER_ROWS)
                g = None
                for h in range(PEER_HEADS):
                    term = jnp.where(s2_ref[lt, h, js, :] >= thr[h], e2_ref[lt, h, js, :] * w1[h], 0.0)
                    g = term if g is None else g + term
                a = a_ref[pl.ds(r0 + rc * PEER_ROWS, PEER_ROWS), ls]
                gelu = 0.5 * a * (1.0 + lax.erf(a * (0.5 ** 0.5)))
                hid_ref[pl.ds(r0 + rc * PEER_ROWS, PEER_ROWS), ls] = (gelu * g).astype(hid_ref.dtype)
        return carry

    lax.fori_loop(0, eb // PEER_NKEYS, body, 0)
    o_ref[...] += jnp.dot(vt_ref[...], hid_ref[...], preferred_element_type=F32)


def _peer_ffn(h2, u, v_t, thr, w1, s2, e2, tb=512, eb=1024):
    t, d = h2.shape
    ne = u.shape[0]
    tb, eb = min(tb, t), min(eb, ne)
    assert t % tb == 0 and tb % LANE == 0 and ne % eb == 0 and eb % PEER_NKEYS == 0
    ni = eb // PEER_NKEYS
    nlt = tb // LANE
    tok = pl.BlockSpec((nlt, PEER_HEADS, PEER_NKEYS, LANE), lambda i, e: (i, 0, 0, 0))
    sel = pl.BlockSpec((nlt, PEER_HEADS, ni, LANE), lambda i, e: (i, 0, e, 0))
    return pl.pallas_call(
        functools.partial(_peer_ffn_kernel, eb=eb),
        grid=(t // tb, ne // eb),
        in_specs=[pl.BlockSpec((tb, d), lambda i, e: (i, 0)),
                  pl.BlockSpec((eb, d), lambda i, e: (e, 0)),
                  pl.BlockSpec((d, eb), lambda i, e: (0, e)),
                  sel, sel, tok, tok],
        out_specs=pl.BlockSpec((d, tb), lambda i, e: (0, i)),
        out_shape=jax.ShapeDtypeStruct((d, t), F32),
        scratch_shapes=[pltpu.VMEM((eb, tb), F32), pltpu.VMEM((eb, tb), BF16)],
        compiler_params=_params("arbitrary", "arbitrary"),
        name="peer_ffn",
    )(h2, u, v_t, thr, w1, s2, e2)


def _final_kernel(x1_ref, pt_ref, g_ref, o_ref):
    o_ref[...] = _rms(x1_ref[...] + pt_ref[...].T, g_ref[...])


def _final(x1, peer_t, g, tb=512):
    t, d = x1.shape
    tb = min(tb, t)
    assert t % tb == 0
    return pl.pallas_call(
        _final_kernel,
        grid=(t // tb,),
        in_specs=[pl.BlockSpec((tb, d), lambda i: (i, 0)),
                  pl.BlockSpec((d, tb), lambda i: (0, i)),
                  pl.BlockSpec((1, d), lambda i: (0, 0))],
        out_specs=pl.BlockSpec((tb, d), lambda i: (i, 0)),
        out_shape=jax.ShapeDtypeStruct((t, d), F32),
        compiler_params=_params("arbitrary"),
        name="final_norm",
    )(x1, peer_t, g.reshape(1, d))


def kernel(x, mem, g_mix, w_in, b_gate, lb_fwd_logits, lb_bwd_logits, g_hgrn, w_mem_kv, g_mem, w_up, w_out,
           g_ffn, w_peer_q, peer_sub_keys, peer_u, peer_v, g_final):
    batch, seq, d = x.shape
    assert d == D_MODEL and w_in.shape[0] == 1, "one layer"
    t = batch * seq
    xt = x.reshape(t, d)
    w = w_in[0]

    proj_hg = _proj(xt, g_mix[0], w, 0, HG_COLS, HG_WIDTH, 1, F32)
    qkv, mq = _attn_proj(xt, g_mix[0], w, HG_COLS, batch, seq)
    gates = _proj(xt, g_mix[0], w, HG_COLS + AT_COLS, GATE_COLS, 512, 2, BF16, bias=b_gate[0])

    o_f, o_b = _hgrn_scan(proj_hg, lb_fwd_logits, lb_bwd_logits, batch, seq)
    dil = [_dilated_group(qkv[gi], gi, batch, seq) for gi in range(len(DIL_GROUPS))]
    kv = _proj(mem.reshape(-1, d), g_mem[0], w_mem_kv[0], 0, 2 * MEM_WIDTH, 512, 2, BF16)
    o_mem = _mem_attn(mq, kv, batch, seq)

    x1, h2 = _merge(o_f, o_b, proj_hg, g_hgrn[0], dil, o_mem, gates, w_up[0].astype(BF16), w_out[0].astype(BF16),
                    xt, g_ffn[0])

    thr, w1, s2, e2, u_b, v_t = _peer_topk(h2, w_peer_q[0].T.astype(BF16), peer_sub_keys[0].astype(BF16),
                                           peer_u[0], peer_v[0])
    peer_t = _peer_ffn(h2, u_b, v_t, thr, w1, s2, e2)
    return _final(x1, peer_t, g_final).reshape(batch, seq, d)
```

```python
import functools

import numpy as np
import jax
import jax.numpy as jnp
from jax import lax
from jax.experimental import pallas as pl
from jax.experimental.pallas import tpu as pltpu

F32 = jnp.float32
BF16 = jnp.bfloat16

EPS = 1e-6
LANE = 128
VMEM_LIMIT = 56 * 1024 * 1024

D_MODEL = 2048
HG_HEADS, HG_DK = 6, 128
HG_WIDTH = HG_HEADS * HG_DK
HG_CHUNK = 64
DIL_GROUPS = ((128, 1), (512, 4), (2048, 16))
DIL_HPG, DIL_HD = 4, 64
DIL_GW = DIL_HPG * DIL_HD
DIL_WIDTH = DIL_GW * len(DIL_GROUPS)
DIL_HEADS = DIL_HPG * len(DIL_GROUPS)
ALIBI_MAX = 8.0
MEM_HEADS, MEM_HD = 4, 128
MEM_WIDTH = MEM_HEADS * MEM_HD
N_BRANCH = 3
HG_COLS = 5 * HG_WIDTH
AT_COLS = 3 * DIL_WIDTH + MEM_WIDTH
GATE_COLS = N_BRANCH * D_MODEL
PEER_HEADS, PEER_NKEYS, PEER_HALF, PEER_TOPK = 8, 128, 128, 16

NT_DIMS = (((1,), (1,)), ((), ()))
TN_DIMS = (((0,), (0,)), ((), ()))


def _params(*sem, vmem_limit=VMEM_LIMIT):
    return pltpu.CompilerParams(dimension_semantics=sem, vmem_limit_bytes=vmem_limit)


def _rms(x, g):
    return x * lax.rsqrt(jnp.mean(x * x, axis=-1, keepdims=True) + EPS) * g


def _normed_input(x_ref, g_ref, h_ref):
    @pl.when(pl.program_id(1) == 0)
    def _():
        h_ref[...] = _rms(x_ref[...], g_ref[...]).astype(h_ref.dtype)

    return h_ref[...]


def _proj_kernel(x_ref, g_ref, *rest, nw, sigmoid_bias):
    w_refs, rest = rest[:nw], rest[nw:]
    b_refs, rest = (rest[:nw], rest[nw:]) if sigmoid_bias else ((), rest)
    o_ref, h_ref = rest
    h = _normed_input(x_ref, g_ref, h_ref)
    wb = w_refs[0].shape[1]
    for k in range(nw):
        acc = jnp.dot(h, w_refs[k][...].astype(BF16), preferred_element_type=F32)
        if sigmoid_bias:
            acc = jax.nn.sigmoid(acc + b_refs[k][...])
        o_ref[:, k * wb:(k + 1) * wb] = acc.astype(o_ref.dtype)


def _proj(x, g, w, col0, ncols, wb, nw, out_dtype, bias=None, tm=1024):
    t, d = x.shape
    tm = min(tm, t)
    tn = wb * nw
    assert t % tm == 0 and ncols % tn == 0 and col0 % wb == 0
    off = col0 // wb
    in_specs = [pl.BlockSpec((tm, d), lambda i, j: (i, 0)), pl.BlockSpec((1, d), lambda i, j: (0, 0))]
    in_specs += [pl.BlockSpec((d, wb), lambda i, j, k=k: (0, off + nw * j + k)) for k in range(nw)]
    args = [x, g.reshape(1, d)] + [w] * nw
    if bias is not None:
        in_specs += [pl.BlockSpec((1, wb), lambda i, j, k=k: (0, nw * j + k)) for k in range(nw)]
        args += [bias.reshape(1, ncols)] * nw
    return pl.pallas_call(
        functools.partial(_proj_kernel, nw=nw, sigmoid_bias=bias is not None),
        grid=(t // tm, ncols // tn),
        in_specs=in_specs,
        out_specs=pl.BlockSpec((tm, tn), lambda i, j: (i, j)),
        out_shape=jax.ShapeDtypeStruct((t, ncols), out_dtype),
        scratch_shapes=[pltpu.VMEM((tm, d), BF16)],
        compiler_params=_params("arbitrary", "arbitrary"),
        name="proj",
    )(*args)


def _attn_proj_kernel(x_ref, g_ref, wq_ref, wk_ref, wv_ref, *rest, dils):
    out_refs, mq_ref, h_ref, scr_ref = rest[:len(dils)], rest[len(dils)], rest[-2], rest[-1]
    j = pl.program_id(1)
    h = _normed_input(x_ref, g_ref, h_ref)
    acc = [jnp.dot(h, w[...].astype(BF16), preferred_element_type=F32) for w in (wq_ref, wk_ref, wv_ref)]
    tm = x_ref.shape[0]
    for gi, r in enumerate(dils):
        @pl.when(j == gi)
        def _(gi=gi, r=r):
            o_ref = out_refs[gi]
            if r == 1:
                for k in range(3):
                    o_ref[0, :, k * DIL_GW:(k + 1) * DIL_GW] = acc[k].astype(o_ref.dtype)
                return
            for k in range(3):
                for c in range(DIL_GW // LANE):
                    scr_ref[(DIL_GW // LANE) * k + c] = acc[k][:, c * LANE:(c + 1) * LANE]
            for rho in range(r):
                for c in range(3 * DIL_GW // LANE):
                    rows = scr_ref[c, pl.ds(rho, tm // r, stride=r), :]
                    o_ref[rho, :, c * LANE:(c + 1) * LANE] = rows.astype(o_ref.dtype)

    @pl.when(j == len(dils))
    def _():
        for k in range(2):
            mq_ref[:, k * DIL_GW:(k + 1) * DIL_GW] = acc[k].astype(mq_ref.dtype)


def _attn_proj(x, g, w, col0, batch, seq, tm=1024):
    t, d = x.shape
    tm = min(tm, seq)
    dils = tuple(r for _, r in DIL_GROUPS)
    ng = len(dils)
    assert seq % tm == 0 and col0 % DIL_GW == 0 and MEM_WIDTH == 2 * DIL_GW and all(tm % (16 * r) == 0 for r in dils)
    off = col0 // DIL_GW
    nbs = seq // tm

    def w_spec(which):
        return pl.BlockSpec((d, DIL_GW), lambda i, j: (0, off + jnp.where(
            j < ng, which * ng + j, 3 * ng + jnp.minimum(which, 1))))

    out_specs = [pl.BlockSpec((None, r, tm // r, 3 * DIL_GW), lambda i, j: (i // nbs, 0, i % nbs, 0)) for r in dils]
    out_shape = [jax.ShapeDtypeStruct((batch, r, seq // r, 3 * DIL_GW), BF16) for r in dils]
    out_specs.append(pl.BlockSpec((tm, MEM_WIDTH), lambda i, j: (i, 0)))
    out_shape.append(jax.ShapeDtypeStruct((t, MEM_WIDTH), BF16))
    outs = pl.pallas_call(
        functools.partial(_attn_proj_kernel, dils=dils),
        grid=(t // tm, ng + 1),
        in_specs=[pl.BlockSpec((tm, d), lambda i, j: (i, 0)), pl.BlockSpec((1, d), lambda i, j: (0, 0)),
                  w_spec(0), w_spec(1), w_spec(2)],
        out_specs=out_specs,
        out_shape=out_shape,
        scratch_shapes=[pltpu.VMEM((tm, d), BF16), pltpu.VMEM((3 * DIL_GW // LANE, tm, LANE), F32)],
        compiler_params=_params("arbitrary", "arbitrary"),
        name="attn_proj",
    )(x, g.reshape(1, d), w, w, w)
    return outs[:ng], outs[ng]


def _hgrn_levels(c):
    ms = []
    m = 1
    while m <= c:
        ms.append(m)
        m *= 2
    return ms


def _hgrn_constants(c):
    t = np.arange(c)[:, None]
    r = np.arange(c)[None, :]
    sums = {"f": [], "b": []}
    masks = {"f": [np.eye(c)], "b": [np.eye(c)]}
    for m in _hgrn_levels(c):
        same = (t // m) == (r // m)
        if m > 1:
            sums["f"] += [same & (r <= t), same & (r > t)]
            sums["b"] += [same & (r >= t), same & (r < t)]
        if m < c:
            blk = (t // (2 * m)) == (r // (2 * m))
            t_hi, r_hi = (t // m) % 2 == 1, (r // m) % 2 == 1
            masks["f"].append(blk & t_hi & ~r_hi)
            masks["b"].append(blk & ~t_hi & r_hi)
    out = []
    for d in ("f", "b"):
        out.append(jnp.asarray(np.concatenate(sums[d], axis=0).astype(np.float32), BF16))
        out.append(jnp.asarray(np.stack(masks[d]).astype(np.float32), F32))
    return out


def _split2(x):
    hi = x.astype(BF16)
    lo = (x - hi.astype(F32)).astype(BF16)
    return hi, lo


def _hgrn_kernel(qf_ref, vf_ref, zf_ref, qb_ref, vb_ref, zb_ref, lbf_ref, lbb_ref,
                 sf_ref, mf_ref, sb_ref, mb_ref, of_ref, ob_ref, stf_ref, stb_ref, *, c, tb):
    @pl.when(pl.program_id(2) == 0)
    def _():
        stf_ref[...] = jnp.zeros_like(stf_ref)
        stb_ref[...] = jnp.zeros_like(stb_ref)

    def lower_bound(ref):
        lg = ref[...]
        ex = jnp.exp(lg - jnp.max(lg, axis=0, keepdims=True))
        return ex[0:1, :] / jnp.sum(ex, axis=0, keepdims=True)

    n = tb // c
    nlev = len(_hgrn_levels(c))
    items = []
    for ci in range(n):
        items.append((qf_ref, vf_ref, zf_ref, of_ref, sf_ref, mf_ref, "f", slice(ci * c, (ci + 1) * c), c - 1))
        items.append((qb_ref, vb_ref, zb_ref, ob_ref, sb_ref, mb_ref, "b", slice((n - 1 - ci) * c, (n - ci) * c), 0))
    lb = {"f": lower_bound(lbf_ref), "b": lower_bound(lbb_ref)}

    q, k, f, vb, kb = [], [], [], [], []
    for q_ref, v_ref, z_ref, _, _, _, d, sl, _ in items:
        fi = lb[d] + (1.0 - lb[d]) * jax.nn.sigmoid(z_ref[sl, :])
        f.append(fi)
        k.append(1.0 - fi)
        q.append(q_ref[sl, :])
        vb.append(v_ref[sl, :].astype(BF16))
        kb.append(k[-1].astype(BF16))
    e = []
    for it, fi in zip(items, f):
        cs2 = jnp.dot(it[4][...], jnp.concatenate(_split2(jnp.log(fi)), axis=1), preferred_element_type=F32)
        e.append(jnp.exp(cs2[:, :LANE] + cs2[:, LANE:]))
    a = [it[5][0] * lax.dot_general(qi.astype(BF16), kbi, NT_DIMS, preferred_element_type=F32)
         for it, qi, kbi in zip(items, q, kb)]
    for li in range(nlev - 1):
        for x, it in enumerate(items):
            if li == 0:
                qd, kd = (q[x] * f[x]).astype(BF16), kb[x]
            else:
                qd = (q[x] * e[x][2 * (li - 1) * c:(2 * li - 1) * c]).astype(BF16)
                kd = (k[x] * e[x][(2 * li - 1) * c:2 * li * c]).astype(BF16)
            a[x] = a[x] + it[5][li + 1] * lax.dot_general(qd, kd, NT_DIMS, preferred_element_type=F32)
    o = [jnp.dot(ai.astype(BF16), vbi, preferred_element_type=F32) for ai, vbi in zip(a, vb)]
    li = nlev - 1
    qd = [(q[x] * e[x][2 * (li - 1) * c:(2 * li - 1) * c]).astype(BF16) for x in range(len(items))]
    kv = [lax.dot_general(vb[x], (k[x] * e[x][(2 * li - 1) * c:2 * li * c]).astype(BF16), TN_DIMS,
                          preferred_element_type=F32) for x in range(len(items))]
    st = {"f": stf_ref[...], "b": stb_ref[...]}
    for x, it in enumerate(items):
        d, sl, total_row = it[6], it[7], it[8]
        it[3][sl, :] = o[x] + lax.dot_general(qd[x], st[d].astype(BF16), NT_DIMS, preferred_element_type=F32)
        decay = e[x][2 * (li - 1) * c + total_row:2 * (li - 1) * c + total_row + 1, :]
        st[d] = st[d] * decay + kv[x]
    stf_ref[...] = st["f"]
    stb_ref[...] = st["b"]


def _hgrn_scan(proj, lbf_logits, lbb_logits, batch, seq, tb=512, c=HG_CHUNK):
    t = batch * seq
    tb = min(tb, seq)
    nb = seq // tb
    assert seq % tb == 0 and tb % c == 0
    hh = HG_HEADS
    sums_f, masks_f, sums_b, masks_b = _hgrn_constants(c)
    nrow = lbf_logits.shape[0]

    def fwd(col):
        return pl.BlockSpec((tb, HG_DK), lambda b, h, i: (b * nb + i, col * hh + h))

    def bwd(col):
        return pl.BlockSpec((tb, HG_DK), lambda b, h, i: (b * nb + nb - 1 - i, col * hh + h))

    def const(a):
        return pl.BlockSpec(a.shape, lambda b, h, i: (0,) * a.ndim)

    lb_spec = pl.BlockSpec((nrow, HG_DK), lambda b, h, i: (0, h))
    out_sds = jax.ShapeDtypeStruct((t, HG_WIDTH), F32)
    return pl.pallas_call(
        functools.partial(_hgrn_kernel, c=c, tb=tb),
        grid=(batch, hh, nb),
        in_specs=[fwd(0), fwd(1), fwd(2), bwd(0), bwd(1), bwd(3), lb_spec, lb_spec,
                  const(sums_f), const(masks_f), const(sums_b), const(masks_b)],
        out_specs=[pl.BlockSpec((tb, HG_DK), lambda b, h, i: (b * nb + i, h)),
                   pl.BlockSpec((tb, HG_DK), lambda b, h, i: (b * nb + nb - 1 - i, h))],
        out_shape=[out_sds, out_sds],
        scratch_shapes=[pltpu.VMEM((HG_DK, HG_DK), F32), pltpu.VMEM((HG_DK, HG_DK), F32)],
        compiler_params=_params("arbitrary", "arbitrary", "arbitrary"),
        name="hgrn_scan",
    )(proj, proj, proj, proj, proj, proj, lbf_logits, lbb_logits, sums_f, masks_f, sums_b, masks_b)


def _dil_kernel(q_ref, k_ref, v_ref, o_ref, lse_ref, *, lq, sub, length, win, n_side, slopes):
    for sb in range(lq // sub):
        rows = slice(sb * sub, (sb + 1) * sub)
        q0 = pl.program_id(2) * lq + sb * sub
        start = pl.multiple_of(jnp.clip(q0 - n_side, 0, length - win), n_side)
        q = q_ref[rows, :]
        kw = k_ref[pl.ds(start, win), :]
        vw = v_ref[pl.ds(start, win), :]
        qpos = q0 + lax.broadcasted_iota(jnp.int32, (sub, win), 0)
        kpos = start + lax.broadcasted_iota(jnp.int32, (sub, win), 1)
        dist = jnp.abs(kpos - qpos)
        valid = dist <= n_side
        distf = dist.astype(F32)
        outs, lses = [], []
        for i, slope in enumerate(slopes):
            cs = slice(i * DIL_HD, (i + 1) * DIL_HD)
            s = lax.dot_general(q[:, cs], kw[:, cs], NT_DIMS, preferred_element_type=F32)
            s = s * (DIL_HD ** -0.5) - slope * distf
            s = jnp.where(valid, s, -1e30)
            m = jnp.max(s, axis=-1, keepdims=True)
            p = jnp.exp(s - m)
            l = jnp.sum(p, axis=-1, keepdims=True)
            o = jnp.dot(p.astype(BF16), vw[:, cs], preferred_element_type=F32) / l
            outs.append(o)
            lses.append(jnp.broadcast_to(m + jnp.log(l), (sub, DIL_HD)))
        o_ref[rows, :] = jnp.concatenate(outs, axis=1)
        lse_ref[rows, :] = jnp.concatenate(lses, axis=1)


def _dilated_group(qkv, gi, batch, seq):
    window, r = DIL_GROUPS[gi]
    n_side = (window // 2) // r
    t = batch * seq
    length = seq // r
    lq = min(256, length)
    sub = min(128, lq)
    if min(length, sub + 2 * n_side) == min(length, lq + 2 * n_side):
        sub = lq
    win = min(length, sub + 2 * n_side)
    nq = length // lq
    assert seq % r == 0 and length % lq == 0 and lq % sub == 0 and sub % n_side == 0
    assert (length - win) % n_side == 0
    heads = np.arange(gi * DIL_HPG, (gi + 1) * DIL_HPG)
    slopes = tuple(float(s) * r for s in np.exp2(-ALIBI_MAX * (heads + 1) / DIL_HEADS))
    x = qkv.reshape(batch * r * length, 3 * DIL_GW)

    def kv_spec(which):
        return pl.BlockSpec((length, DIL_GW), lambda b, rho, qi: (b * r + rho, which))

    out_spec = pl.BlockSpec((lq, DIL_GW), lambda b, rho, qi: (b * nq + qi, rho))
    out_sds = jax.ShapeDtypeStruct((t // r, r * DIL_GW), F32)
    o, lse = pl.pallas_call(
        functools.partial(_dil_kernel, lq=lq, sub=sub, length=length, win=win, n_side=n_side, slopes=slopes),
        grid=(batch, r, nq),
        in_specs=[pl.BlockSpec((lq, DIL_GW), lambda b, rho, qi: ((b * r + rho) * nq + qi, 0)),
                  kv_spec(1), kv_spec(2)],
        out_specs=[out_spec, out_spec],
        out_shape=[out_sds, out_sds],
        compiler_params=_params("arbitrary", "arbitrary", "arbitrary"),
        name=f"dilated_attn_{gi}",
    )(x, x, x)
    return o.reshape(t, DIL_GW), lse.reshape(t, DIL_GW)


def _mem_attn_kernel(q_ref, k_ref, v_ref, o_ref):
    q, k, v = q_ref[...], k_ref[...], v_ref[...]
    outs = []
    for i in range(q.shape[1] // MEM_HD):
        cs = slice(i * MEM_HD, (i + 1) * MEM_HD)
        s = lax.dot_general(q[:, cs], k[:, cs], NT_DIMS, preferred_element_type=F32) * (MEM_HD ** -0.5)
        p = jnp.exp(s - jnp.max(s, axis=-1, keepdims=True))
        l = jnp.sum(p, axis=-1, keepdims=True)
        outs.append(jnp.dot(p.astype(BF16), v[:, cs], preferred_element_type=F32) / l)
    o_ref[...] = jnp.concatenate(outs, axis=1).astype(o_ref.dtype)


def _mem_attn(mq, kv, batch, seq, tq=1024):
    t = batch * seq
    tq = min(tq, seq)
    nq = seq // tq
    m = kv.shape[0] // batch
    bw = 2 * MEM_HD
    npair = MEM_WIDTH // bw
    assert seq % tq == 0
    return pl.pallas_call(
        _mem_attn_kernel,
        grid=(batch, nq, npair),
        in_specs=[pl.BlockSpec((tq, bw), lambda b, i, hp: (b * nq + i, hp)),
                  pl.BlockSpec((m, bw), lambda b, i, hp: (b, hp)),
                  pl.BlockSpec((m, bw), lambda b, i, hp: (b, npair + hp))],
        out_specs=pl.BlockSpec((tq, bw), lambda b, i, hp: (b * nq + i, hp)),
        out_shape=jax.ShapeDtypeStruct((t, MEM_WIDTH), BF16),
        compiler_params=_params("arbitrary", "arbitrary", "arbitrary"),
        name="mem_attn",
    )(mq, kv, kv)


def _merge_kernel(of_ref, ob_ref, hg_ref, ghg_ref, od0_ref, od1_ref, od2_ref, l0_ref, l1_ref, l2_ref,
                  om_ref, gate_ref, wup_ref, wout_ref, x_ref, gffn_ref, x1_ref, h2_ref):
    o = of_ref[...] + ob_ref[...]
    ghg = ghg_ref[...]
    parts = [_rms(o[:, h * HG_DK:(h + 1) * HG_DK], ghg) for h in range(HG_HEADS)]
    gz = hg_ref[...]
    o_hg = (jnp.concatenate(parts, axis=1) * (gz * jax.nn.sigmoid(gz))).astype(BF16)

    l0, l1, l2 = l0_ref[...], l1_ref[...], l2_ref[...]
    mx = jnp.maximum(jnp.maximum(l0, l1), l2)
    e0, e1, e2 = jnp.exp(l0 - mx), jnp.exp(l1 - mx), jnp.exp(l2 - mx)
    inv = 1.0 / (e0 + e1 + e2)
    o_dil = jnp.concatenate([od0_ref[...] * (e0 * inv), od1_ref[...] * (e1 * inv), od2_ref[...] * (e2 * inv)],
                            axis=1).astype(BF16)

    d = x_ref.shape[1]
    y_hg = jnp.dot(o_hg, wup_ref[0:HG_WIDTH, :], preferred_element_type=F32)
    merged = gate_ref[:, 0:d].astype(F32) * y_hg
    y_dil = jnp.dot(o_dil, wup_ref[HG_WIDTH:HG_WIDTH + DIL_WIDTH, :], preferred_element_type=F32)
    merged = merged + gate_ref[:, d:2 * d].astype(F32) * y_dil
    y_mem = jnp.dot(om_ref[...], wup_ref[HG_WIDTH + DIL_WIDTH:, :], preferred_element_type=F32)
    merged = merged + gate_ref[:, 2 * d:].astype(F32) * y_mem
    x1 = x_ref[...] + jnp.dot(merged.astype(BF16), wout_ref[...], preferred_element_type=F32)
    x1_ref[...] = x1
    h2_ref[...] = _rms(x1, gffn_ref[...]).astype(h2_ref.dtype)


def _merge(o_f, o_b, proj_hg, g_hg, dil, o_mem, gates, w_up, w_out, x, g_ffn, tm=256):
    t, d = x.shape
    tm = min(tm, t)
    assert t % tm == 0
    row = lambda w, col=0: pl.BlockSpec((tm, w), lambda i: (i, col))
    const = lambda a: pl.BlockSpec(a.shape, lambda i: (0, 0), pipeline_mode=pl.Buffered(1))
    (od0, l0), (od1, l1), (od2, l2) = dil
    g_hg2, g_ffn2 = g_hg.reshape(1, HG_DK), g_ffn.reshape(1, d)
    return pl.pallas_call(
        _merge_kernel,
        grid=(t // tm,),
        in_specs=[row(HG_WIDTH), row(HG_WIDTH), row(HG_WIDTH, 4), const(g_hg2),
                  row(DIL_GW), row(DIL_GW), row(DIL_GW), row(DIL_GW), row(DIL_GW), row(DIL_GW),
                  row(MEM_WIDTH), row(GATE_COLS), const(w_up), const(w_out), row(d), const(g_ffn2)],
        out_specs=[row(d), row(d)],
        out_shape=[jax.ShapeDtypeStruct((t, d), F32), jax.ShapeDtypeStruct((t, d), BF16)],
        compiler_params=_params("arbitrary"),
        name="merge",
    )(o_f, o_b, proj_hg, g_hg2, od0, od1, od2, l0, l1, l2, o_mem, gates, w_up, w_out, x, g_ffn2)


N_TOP = PEER_TOPK + 1
TOP_ROWS = 24


SUBLANES = 8


def _sorting_network(n):
    pairs, p = [], 1
    while p < n:
        k = p
        while k >= 1:
            for j in range(k % p, n - k, 2 * k):
                for i in range(min(k, n - j - k)):
                    if (i + j) // (2 * p) == (i + j + k) // (2 * p):
                        pairs.append((i + j, i + j + k))
            k //= 2
        p *= 2
    return pairs


def _pop_largest(lists, count):
    lists = [list(v) for v in lists]
    rows = []
    for r in range(count):
        head = functools.reduce(jnp.maximum, [v[0] for v in lists])
        m = jnp.max(head, axis=0, keepdims=True)
        rows.append(m)
        need = count - r - 1
        for v in lists:
            hit = v[0] == m
            for k in range(min(need, len(v) - 1)):
                v[k] = jnp.where(hit, v[k + 1], v[k])
            if need >= len(v):
                v[-1] = jnp.where(hit, -jnp.inf, v[-1])
    return rows


def _top_values(s):
    n, tb = s.shape
    groups = n // SUBLANES
    v = [s[SUBLANES * k:SUBLANES * (k + 1), :] for k in range(groups)]
    for i, j in _sorting_network(groups):
        v[i], v[j] = jnp.maximum(v[i], v[j]), jnp.minimum(v[i], v[j])
    rows = _pop_largest([v], N_TOP)
    rid = lax.broadcasted_iota(jnp.int32, (TOP_ROWS, tb), 0)
    top = jnp.full((TOP_ROWS, tb), -jnp.inf, F32)
    for r, m in enumerate(rows):
        top = jnp.where(rid == r, m, top)
    return top, rows


def _peer_topk_kernel(h2_ref, wq_ref, keys_ref, u_ref, v_ref, thr_ref, w1_ref, s2_ref, e2_ref, ub_ref, vt_ref):
    ub_ref[...] = u_ref[...].astype(ub_ref.dtype)
    vt_ref[...] = v_ref[...].T.astype(vt_ref.dtype)
    h2 = h2_ref[...]
    qd = 2 * PEER_HALF
    for h in range(PEER_HEADS):
        q_t = lax.dot_general(wq_ref[h * qd:(h + 1) * qd, :], h2, NT_DIMS, preferred_element_type=F32)
        s1 = jnp.dot(keys_ref[0], q_t[:PEER_HALF].astype(BF16), preferred_element_type=F32)
        s2 = jnp.dot(keys_ref[1], q_t[PEER_HALF:].astype(BF16), preferred_element_type=F32)
        top1, rows1 = _top_values(s1)
        top2, rows2 = _top_values(s2)
        grp = lambda t, k: t[SUBLANES * k:SUBLANES * (k + 1)]
        cands = [[rows1[0] + grp(top2, k) for k in range(TOP_ROWS // SUBLANES)],
                 [rows1[a] + grp(top2, 0) for a in range(1, SUBLANES)],
                 [grp(top1, k) + rows2[0] for k in range(1, TOP_ROWS // SUBLANES)]]
        best = _pop_largest(cands, N_TOP)
        z = jnp.ones_like(best[0])
        for r in range(1, PEER_TOPK):
            z = z + jnp.exp(best[r] - best[0])
        tau = 0.5 * (best[PEER_TOPK - 1] + best[PEER_TOPK])
        outs = ((thr_ref, tau - s1),
                (w1_ref, jnp.exp(s1 - rows1[0]) / z),
                (s2_ref, s2),
                (e2_ref, jnp.exp(s2 - rows2[0])))
        for ref, val in outs:
            for lt in range(ref.shape[0]):
                ref[lt, h] = val[:, lt * LANE:(lt + 1) * LANE]


def _peer_topk(h2, wq_t, keys, u, v, tb=256):
    t, d = h2.shape
    ne = u.shape[0]
    tb = min(tb, t)
    steps = t // tb
    assert t % tb == 0 and tb % LANE == 0 and ne % steps == 0
    rows = ne // steps
    spec = pl.BlockSpec((tb // LANE, PEER_HEADS, PEER_NKEYS, LANE), lambda i: (i, 0, 0, 0))
    sds = jax.ShapeDtypeStruct((t // LANE, PEER_HEADS, PEER_NKEYS, LANE), F32)
    return pl.pallas_call(
        _peer_topk_kernel,
        grid=(steps,),
        in_specs=[pl.BlockSpec((tb, d), lambda i: (i, 0)),
                  pl.BlockSpec(wq_t.shape, lambda i: (0, 0), pipeline_mode=pl.Buffered(1)),
                  pl.BlockSpec(keys.shape, lambda i: (0, 0, 0)),
                  pl.BlockSpec((rows, d), lambda i: (i, 0)),
                  pl.BlockSpec((rows, d), lambda i: (i, 0))],
        out_specs=[spec] * 4 + [pl.BlockSpec((rows, d), lambda i: (i, 0)), pl.BlockSpec((d, rows), lambda i: (0, i))],
        out_shape=[sds] * 4 + [jax.ShapeDtypeStruct((ne, d), BF16), jax.ShapeDtypeStruct((d, ne), BF16)],
        compiler_params=_params("arbitrary"),
        name="peer_topk",
    )(h2, wq_t, keys, u, v)


PEER_ROWS = 16
PEER_VMEM_LIMIT = 61 * 1024 * 1024


def _peer_ffn_kernel(h2_ref, u_ref, vt_ref, thr_ref, w1_ref, s2_ref, e2_ref, o_ref, a_ref, hid_ref, *, eb):
    @pl.when(pl.program_id(1) == 0)
    def _():
        o_ref[...] = jnp.zeros_like(o_ref)

    tb = h2_ref.shape[0]
    a_ref[...] = lax.dot_general(u_ref[...], h2_ref[...], NT_DIMS, preferred_element_type=F32)

    def body(ii, carry):
        r0 = pl.multiple_of(ii * PEER_NKEYS, PEER_NKEYS)
        for lt in range(tb // LANE):
            ls = slice(lt * LANE, (lt + 1) * LANE)
            thr = [jnp.broadcast_to(thr_ref[lt, h, pl.ds(ii, 1), :], (PEER_ROWS, LANE)) for h in range(PEER_HEADS)]
            w1 = [jnp.broadcast_to(w1_ref[lt, h, pl.ds(ii, 1), :], (PEER_ROWS, LANE)) for h in range(PEER_HEADS)]
            for rc in range(PEER_NKEYS // PEER_ROWS):
                js = slice(rc * PEER_ROWS, (rc + 1) * PEER_ROWS)
                g = None
                for h in range(PEER_HEADS):
                    term = jnp.where(s2_ref[lt, h, js, :] >= thr[h], e2_ref[lt, h, js, :] * w1[h], 0.0)
                    g = term if g is None else g + term
                a = a_ref[pl.ds(r0 + rc * PEER_ROWS, PEER_ROWS), ls]
                gelu = 0.5 * a * (1.0 + lax.erf(a * (0.5 ** 0.5)))
                hid_ref[pl.ds(r0 + rc * PEER_ROWS, PEER_ROWS), ls] = (gelu * g).astype(hid_ref.dtype)
        return carry

    lax.fori_loop(0, eb // PEER_NKEYS, body, 0)
    o_ref[...] += jnp.dot(vt_ref[...], hid_ref[...], preferred_element_type=F32)


def _peer_ffn(h2, u, v_t, thr, w1, s2, e2, tb=1024, eb=1024):
    t, d = h2.shape
    ne = u.shape[0]
    tb, eb = min(tb, t), min(eb, ne)
    assert t % tb == 0 and tb % LANE == 0 and ne % eb == 0 and eb % PEER_NKEYS == 0
    ni = eb // PEER_NKEYS
    nlt = tb // LANE
    once = pl.Buffered(1)
    tok = pl.BlockSpec((nlt, PEER_HEADS, PEER_NKEYS, LANE), lambda i, e: (i, 0, 0, 0), pipeline_mode=once)
    sel = pl.BlockSpec((nlt, PEER_HEADS, ni, LANE), lambda i, e: (i, 0, e, 0))
    return pl.pallas_call(
        functools.partial(_peer_ffn_kernel, eb=eb),
        grid=(t // tb, ne // eb),
        in_specs=[pl.BlockSpec((tb, d), lambda i, e: (i, 0), pipeline_mode=once),
                  pl.BlockSpec((eb, d), lambda i, e: (e, 0)),
                  pl.BlockSpec((d, eb), lambda i, e: (0, e)),
                  sel, sel, tok, tok],
        out_specs=pl.BlockSpec((d, tb), lambda i, e: (0, i)),
        out_shape=jax.ShapeDtypeStruct((d, t), F32),
        scratch_shapes=[pltpu.VMEM((eb, tb), F32), pltpu.VMEM((eb, tb), BF16)],
        compiler_params=_params("arbitrary", "arbitrary", vmem_limit=PEER_VMEM_LIMIT),
        name="peer_ffn",
    )(h2, u, v_t, thr, w1, s2, e2)


def _final_kernel(x1_ref, pt_ref, g_ref, o_ref):
    o_ref[...] = _rms(x1_ref[...] + pt_ref[...].T, g_ref[...])


def _final(x1, peer_t, g, tb=512):
    t, d = x1.shape
    tb = min(tb, t)
    assert t % tb == 0
    return pl.pallas_call(
        _final_kernel,
        grid=(t // tb,),
        in_specs=[pl.BlockSpec((tb, d), lambda i: (i, 0)),
                  pl.BlockSpec((d, tb), lambda i: (0, i)),
                  pl.BlockSpec((1, d), lambda i: (0, 0))],
        out_specs=pl.BlockSpec((tb, d), lambda i: (i, 0)),
        out_shape=jax.ShapeDtypeStruct((t, d), F32),
        compiler_params=_params("arbitrary"),
        name="final_norm",
    )(x1, peer_t, g.reshape(1, d))


def kernel(x, mem, g_mix, w_in, b_gate, lb_fwd_logits, lb_bwd_logits, g_hgrn, w_mem_kv, g_mem, w_up, w_out,
           g_ffn, w_peer_q, peer_sub_keys, peer_u, peer_v, g_final):
    batch, seq, d = x.shape
    assert d == D_MODEL and w_in.shape[0] == 1, "one layer"
    t = batch * seq
    xt = x.reshape(t, d)
    w = w_in[0]

    proj_hg = _proj(xt, g_mix[0], w, 0, HG_COLS, HG_WIDTH, 1, F32)
    qkv, mq = _attn_proj(xt, g_mix[0], w, HG_COLS, batch, seq)
    gates = _proj(xt, g_mix[0], w, HG_COLS + AT_COLS, GATE_COLS, 512, 2, BF16, bias=b_gate[0])

    o_f, o_b = _hgrn_scan(proj_hg, lb_fwd_logits, lb_bwd_logits, batch, seq)
    dil = [_dilated_group(qkv[gi], gi, batch, seq) for gi in range(len(DIL_GROUPS))]
    kv = _proj(mem.reshape(-1, d), g_mem[0], w_mem_kv[0], 0, 2 * MEM_WIDTH, 512, 2, BF16)
    o_mem = _mem_attn(mq, kv, batch, seq)

    x1, h2 = _merge(o_f, o_b, proj_hg, g_hgrn[0], dil, o_mem, gates, w_up[0].astype(BF16), w_out[0].astype(BF16),
                    xt, g_ffn[0])

    thr, w1, s2, e2, u_b, v_t = _peer_topk(h2, w_peer_q[0].T.astype(BF16), peer_sub_keys[0].astype(BF16),
                                           peer_u[0], peer_v[0])
    peer_t = _peer_ffn(h2, u_b, v_t, thr, w1, s2, e2)
    return _final(x1, peer_t, g_final).reshape(batch, seq, d)
```
